```python
import jax, jax.numpy as jnp
from jax import lax
import numpy as np

D_MODEL = 1024
BATCH = 16
SEQ = 2048
DEPTH = 2

RW_HEADS = 8
RW_HEAD_DIM = 64
RW_WIDTH = RW_HEADS * RW_HEAD_DIM
RW_DECAY_LORA = 64
RW_ICLR_LORA = 64
RW_VRES_LORA = 32
RW_GATE_LORA = 160
RW_GN_EPS = 64e-5
RW_SHIFT_COLS = 3 * RW_WIDTH + RW_DECAY_LORA + RW_ICLR_LORA + RW_GATE_LORA
SSM_HEADS = 8
SSM_HEAD_DIM = 64
SSM_WIDTH = SSM_HEADS * SSM_HEAD_DIM
SSM_GROUPS = 2
SSM_STATE = 128
SSM_CONV = 4
SSM_CHUNK = 128
SSM_CONV_CH = SSM_WIDTH + 2 * SSM_GROUPS * SSM_STATE
SSM_COLS = SSM_WIDTH + SSM_CONV_CH + SSM_HEADS
ATT_Q_HEADS = 8
ATT_KV_HEADS = 2
ATT_HEAD_DIM = 64
ATT_WIDTH = ATT_Q_HEADS * ATT_HEAD_DIM
ATT_KV_WIDTH = ATT_KV_HEADS * ATT_HEAD_DIM
ATT_COLS = ATT_WIDTH + 2 * ATT_KV_WIDTH
WINDOW = 128
N_BRANCHES = 3
GATE_COLS = N_BRANCHES * D_MODEL
IN_COLS = RW_SHIFT_COLS + SSM_COLS + ATT_COLS + GATE_COLS
FFN_HIDDEN = 256 * (-(-8 * D_MODEL // (3 * 256)))
NORM_EPS = 1e-6

kernel_name = "hybrid_rwkv7_mamba2_swa_gated_block"


def _split(t, sizes):
    idx = np.cumsum(sizes)[:-1].tolist()
    return jnp.split(t, idx, axis=-1)


def rms_norm(x, w, eps=NORM_EPS):
    xf = x.astype(jnp.float32)
    y = xf * lax.rsqrt(jnp.mean(xf * xf, axis=-1, keepdims=True) + eps)
    return (y * w.astype(jnp.float32)).astype(x.dtype)


def token_shift(p):
    return jnp.pad(p, ((0, 0), (1, 0), (0, 0)))[:, :-1]


def rwkv7_scan(r, decay, k, v, kk, a):
    f32 = jnp.float32
    Bsz, T, H, N = r.shape
    seq = tuple(jnp.moveaxis(t.astype(f32), 1, 0) for t in (r, decay, k, v, kk, a))

    def step(S, inp):
        r_t, w_t, k_t, v_t, kk_t, a_t = inp
        sa = jnp.einsum('bhvk,bhk->bhv', S, -kk_t)
        S = (S * w_t[:, :, None, :]
             + sa[..., None] * (kk_t * a_t)[:, :, None, :]
             + v_t[..., None] * k_t[:, :, None, :])
        return S, jnp.einsum('bhvk,bhk->bhv', S, r_t)

    S0 = jnp.zeros((Bsz, H, N, N), f32)
    _, y = lax.scan(step, S0, seq)
    return jnp.moveaxis(y, 0, 1)


def rwkv7_branch(p_rw, v_first, vres, mu, w0, w2, a0, a2, g2, k_k, k_a, r_k, gn_w, gn_b):
    Bsz, T, _ = p_rw.shape
    p = p_rw + (token_shift(p_rw) - p_rw) * mu
    r, k, v, lw, la, lg = _split(p, [RW_WIDTH] * 3 + [RW_DECAY_LORA, RW_ICLR_LORA, RW_GATE_LORA])
    w = -jax.nn.softplus(-(w0 + jnp.tanh(lw) @ w2)) - 0.5
    decay = jnp.exp(-jnp.exp(w.astype(jnp.float32)))
    a = jax.nn.sigmoid(a0 + la @ a2)
    g = jax.nn.sigmoid(lg) @ g2
    if vres is None:
        v_first = v
    else:
        lv, v0, v2 = vres
        v = v + (v_first - v) * jax.nn.sigmoid(v0 + lv @ v2)
    heads = lambda t: t.reshape(Bsz, T, RW_HEADS, RW_HEAD_DIM)
    kk = heads(k * k_k).astype(jnp.float32)
    kk = kk / jnp.maximum(jnp.sqrt(jnp.sum(kk * kk, axis=-1, keepdims=True)), 1e-12)
    k = k * (1.0 + (a - 1.0) * k_a)
    rh, kh, vh = heads(r), heads(k), heads(v)
    y = rwkv7_scan(rh, heads(decay), kh, vh, kk, heads(a))
    mean = jnp.mean(y, axis=-1, keepdims=True)
    var = jnp.mean(jnp.square(y - mean), axis=-1, keepdims=True)
    y = (y - mean) * lax.rsqrt(var + RW_GN_EPS)
    y = y.reshape(Bsz, T, RW_WIDTH) * gn_w + gn_b
    bonus = jnp.sum(rh * kh * r_k, axis=-1, keepdims=True) * vh
    y = (y + bonus.reshape(Bsz, T, RW_WIDTH)) * g
    return y.astype(p_rw.dtype), v_first


def segsum(a):
    T = a.shape[-1]
    a_rep = jnp.broadcast_to(a[..., None], a.shape + (T,))
    a_rep = jnp.where(jnp.tril(jnp.ones((T, T), bool), -1), a_rep, 0.0)
    s = jnp.cumsum(a_rep, axis=-2)
    return jnp.where(jnp.tril(jnp.ones((T, T), bool), 0), s, -jnp.inf)


def ssd_chunked(x, a, b, c):
    Bsz, T, H, P = x.shape
    N = b.shape[-1]
    Q = SSM_CHUNK
    nc = T // Q
    x = x.reshape(Bsz, nc, Q, H, P)
    b = b.reshape(Bsz, nc, Q, H, N)
    c = c.reshape(Bsz, nc, Q, H, N)
    a = a.reshape(Bsz, nc, Q, H).transpose(0, 3, 1, 2)
    a_cum = jnp.cumsum(a, axis=-1)
    L = jnp.exp(segsum(a))
    cb = jnp.einsum('bclhn,bcshn->bhcls', c, b)
    y_diag = jnp.einsum('bhcls,bcshp->bclhp', cb * L, x)
    decay_states = jnp.exp(a_cum[..., -1:] - a_cum)
    states = jnp.einsum('bclhn,bhcl,bclhp->bchpn', b, decay_states, x)
    states = jnp.concatenate([jnp.zeros_like(states[:, :1]), states], axis=1)
    chunk_decay = jnp.exp(segsum(jnp.pad(a_cum[..., -1], ((0, 0), (0, 0), (1, 0)))))
    states = jnp.einsum('bhzc,bchpn->bzhpn', chunk_decay, states)[:, :-1]
    y_off = jnp.einsum('bclhn,bchpn,bhcl->bclhp', c, states, jnp.exp(a_cum))
    return (y_diag + y_off).reshape(Bsz, T, H, P)


def mamba2_branch(z, xbc, dt_raw, conv_w, conv_b, dt_bias, a_log, d_skip, norm_w):
    Bsz, T, _ = xbc.shape
    conv = lax.conv_general_dilated(
        xbc, conv_w[:, None, :].astype(xbc.dtype), window_strides=(1,),
        padding=[(SSM_CONV - 1, 0)], dimension_numbers=('NWC', 'WIO', 'NWC'),
        feature_group_count=SSM_CONV_CH)
    xbc = jax.nn.silu(conv + conv_b)
    xs, Bm, Cm = _split(xbc, [SSM_WIDTH, SSM_GROUPS * SSM_STATE, SSM_GROUPS * SSM_STATE])
    xs = xs.reshape(Bsz, T, SSM_HEADS, SSM_HEAD_DIM)
    rep = SSM_HEADS // SSM_GROUPS
    Bh = jnp.repeat(Bm.reshape(Bsz, T, SSM_GROUPS, SSM_STATE), rep, axis=2)
    Ch = jnp.repeat(Cm.reshape(Bsz, T, SSM_GROUPS, SSM_STATE), rep, axis=2)
    dt = jax.nn.softplus((dt_raw + dt_bias).astype(jnp.float32))
    A = -jnp.exp(a_log.astype(jnp.float32))
    y = ssd_chunked(xs * dt[..., None], dt * A, Bh, Ch)
    y = y + xs * d_skip[:, None]
    yz = (y.reshape(Bsz, T, SSM_WIDTH) * jax.nn.silu(z)).astype(jnp.float32)
    yz = yz.reshape(Bsz, T, SSM_GROUPS, SSM_WIDTH // SSM_GROUPS)
    yz = yz * lax.rsqrt(jnp.mean(yz * yz, axis=-1, keepdims=True) + NORM_EPS)
    return (yz.reshape(Bsz, T, SSM_WIDTH) * norm_w).astype(xbc.dtype)


def swa_sink_attention(q, k, v, sinks):
    Bsz, T = q.shape[:2]
    W = WINDOW
    nb = T // W
    rep = ATT_Q_HEADS // ATT_KV_HEADS
    q = q.reshape(Bsz, nb, W, ATT_KV_HEADS, rep, ATT_HEAD_DIM)

    def band(t):
        t = t.reshape(Bsz, nb, W, ATT_KV_HEADS, ATT_HEAD_DIM)
        prev = jnp.pad(t, ((0, 0), (1, 0), (0, 0), (0, 0), (0, 0)))[:, :-1]
        return jnp.concatenate([prev, t], axis=2)

    kb, vb = band(k), band(v)
    s = jnp.einsum('bnqgrd,bnkgd->bgrnqk', q, kb).astype(jnp.float32) * (ATT_HEAD_DIM ** -0.5)
    qi = jnp.arange(W)[:, None]
    kj = jnp.arange(2 * W)[None, :]
    rel = qi + W - kj
    blk = jnp.arange(nb)[:, None, None]
    valid = (rel >= 0) & (rel < W) & (blk * W + kj - W >= 0)
    s = jnp.where(valid, s, -jnp.inf)
    sink = jnp.broadcast_to(
        sinks.astype(jnp.float32).reshape(ATT_KV_HEADS, rep)[None, :, :, None, None, None],
        s.shape[:-1] + (1,))
    p = jax.nn.softmax(jnp.concatenate([s, sink], axis=-1), axis=-1)[..., :-1]
    o = jnp.einsum('bgrnqk,bnkgd->bnqgrd', p.astype(vb.dtype), vb)
    return o.reshape(Bsz, T, ATT_WIDTH)


def swiglu(x, w_gu, w_down):
    g, u = jnp.split(x @ w_gu, 2, axis=-1)
    return (jax.nn.silu(g) * u) @ w_down


def setup_inputs(seed: int = 0) -> dict:
    key = jax.random.key(seed)
    keys = jax.random.split(key, 64)
    counter = iter(range(64))
    f32 = jnp.float32

    def nrm(shape, scale):
        return jax.random.normal(keys[next(counter)], shape, f32) * scale

    def uni(shape, lo, hi):
        return jax.random.uniform(keys[next(counter)], shape, f32, lo, hi)

    L, Lv, D = DEPTH, DEPTH - 1, D_MODEL
    dt_init = jnp.exp(uni((L, SSM_HEADS), float(np.log(1e-3)), float(np.log(1e-1))))
    w0_base = jnp.tile(jnp.linspace(-6.5, -1.5, RW_HEAD_DIM, dtype=f32), RW_HEADS)
    return {
        "x": nrm((BATCH, SEQ, D), 1.0),
        "norm_mix": 1.0 + nrm((L, D), 0.02),
        "w_in": nrm((L, D, IN_COLS), D ** -0.5),
        "rw_mu": uni((L, RW_SHIFT_COLS), 0.0, 1.0),
        "rw_w0": w0_base[None] + nrm((L, RW_WIDTH), 0.1),
        "rw_w2": nrm((L, RW_DECAY_LORA, RW_WIDTH), 0.5 * RW_DECAY_LORA ** -0.5),
        "rw_a0": nrm((L, RW_WIDTH), 0.1),
        "rw_a2": nrm((L, RW_ICLR_LORA, RW_WIDTH), 0.5 * RW_ICLR_LORA ** -0.5),
        "rw_g2": nrm((L, RW_GATE_LORA, RW_WIDTH), RW_GATE_LORA ** -0.5),
        "rw_k_k": 0.85 + nrm((L, RW_WIDTH), 0.02),
        "rw_k_a": 1.0 + nrm((L, RW_WIDTH), 0.02),
        "rw_r_k": nrm((L, RW_HEADS, RW_HEAD_DIM), 0.1),
        "rw_gn_w": 1.0 + nrm((L, RW_WIDTH), 0.02),
        "rw_gn_b": nrm((L, RW_WIDTH), 0.02),
        "rw_vres_down": nrm((Lv, D, RW_VRES_LORA), D ** -0.5),
        "rw_vres_mu": uni((Lv, RW_VRES_LORA), 0.0, 1.0),
        "rw_vres_v0": nrm((Lv, RW_WIDTH), 0.5),
        "rw_vres_v2": nrm((Lv, RW_VRES_LORA, RW_WIDTH), RW_VRES_LORA ** -0.5),
        "ssm_conv_w": nrm((L, SSM_CONV, SSM_CONV_CH), SSM_CONV ** -0.5),
        "ssm_conv_b": nrm((L, SSM_CONV_CH), 0.02),
        "ssm_dt_bias": dt_init + jnp.log(-jnp.expm1(-dt_init)),
        "ssm_a_log": jnp.log(uni((L, SSM_HEADS), 1.0, 16.0)),
        "ssm_d": 1.0 + nrm((L, SSM_HEADS), 0.1),
        "ssm_norm_w": 1.0 + nrm((L, SSM_WIDTH), 0.02),
        "att_sinks": nrm((L, ATT_Q_HEADS), 1.0),
        "gate_b": nrm((L, GATE_COLS), 0.1),
        "w_br_rw": nrm((L, RW_WIDTH, D), RW_WIDTH ** -0.5),
        "w_br_ssm": nrm((L, SSM_WIDTH, D), SSM_WIDTH ** -0.5),
        "w_br_att": nrm((L, ATT_WIDTH, D), ATT_WIDTH ** -0.5),
        "w_out": nrm((L, D, D), D ** -0.5),
        "norm_ffn": 1.0 + nrm((L, D), 0.02),
        "ffn_w_gu": nrm((L, D, 2 * FFN_HIDDEN), D ** -0.5),
        "ffn_w_down": nrm((L, FFN_HIDDEN, D), FFN_HIDDEN ** -0.5),
        "norm_final": 1.0 + nrm((D,), 0.02),
    }


def reference(x, norm_mix, w_in, rw_mu, rw_w0, rw_w2, rw_a0, rw_a2, rw_g2, rw_k_k, rw_k_a,
              rw_r_k, rw_gn_w, rw_gn_b, rw_vres_down, rw_vres_mu, rw_vres_v0, rw_vres_v2,
              ssm_conv_w, ssm_conv_b, ssm_dt_bias, ssm_a_log, ssm_d, ssm_norm_w, att_sinks,
              gate_b, w_br_rw, w_br_ssm, w_br_att, w_out, norm_ffn, ffn_w_gu, ffn_w_down,
              norm_final):
    Bsz, T, _ = x.shape
    v_first = None
    for l in range(DEPTH):
        xn = rms_norm(x, norm_mix[l])
        proj = xn @ w_in[l]
        p_rw, p_ssm, p_att, p_gate = _split(proj, [RW_SHIFT_COLS, SSM_COLS, ATT_COLS, GATE_COLS])
        vres = None
        if l > 0:
            lv = xn @ rw_vres_down[l - 1]
            lv = lv + (token_shift(lv) - lv) * rw_vres_mu[l - 1]
            vres = (lv, rw_vres_v0[l - 1], rw_vres_v2[l - 1])
        y_rw, v_first = rwkv7_branch(p_rw, v_first, vres, rw_mu[l], rw_w0[l], rw_w2[l], rw_a0[l],
                                     rw_a2[l], rw_g2[l], rw_k_k[l], rw_k_a[l], rw_r_k[l],
                                     rw_gn_w[l], rw_gn_b[l])
        z, xbc, dt_raw = _split(p_ssm, [SSM_WIDTH, SSM_CONV_CH, SSM_HEADS])
        y_ssm = mamba2_branch(z, xbc, dt_raw, ssm_conv_w[l], ssm_conv_b[l], ssm_dt_bias[l],
                              ssm_a_log[l], ssm_d[l], ssm_norm_w[l])
        q, k, v = _split(p_att, [ATT_WIDTH, ATT_KV_WIDTH, ATT_KV_WIDTH])
        y_att = swa_sink_attention(q.reshape(Bsz, T, ATT_Q_HEADS, ATT_HEAD_DIM),
                                   k.reshape(Bsz, T, ATT_KV_HEADS, ATT_HEAD_DIM),
                                   v.reshape(Bsz, T, ATT_KV_HEADS, ATT_HEAD_DIM), att_sinks[l])
        g_rw, g_ssm, g_att = _split(jax.nn.sigmoid(p_gate + gate_b[l]), [D_MODEL] * N_BRANCHES)
        merged = (g_rw * (y_rw @ w_br_rw[l]) + g_ssm * (y_ssm @ w_br_ssm[l])
                  + g_att * (y_att @ w_br_att[l]))
        x = x + merged @ w_out[l]
        x = x + swiglu(rms_norm(x, norm_ffn[l]), ffn_w_gu[l], ffn_w_down[l])
    return rms_norm(x, norm_final)
```

```python
import functools

import jax
import jax.numpy as jnp
from jax import lax
from jax.experimental import pallas as pl
from jax.experimental.pallas import tpu as pltpu

F32 = jnp.float32
BF16 = jnp.bfloat16

D_MODEL = 1024
HEAD_DIM = 64
N_HEADS = 8
WIDTH = N_HEADS * HEAD_DIM
LANES = 128
N_PAIRS = WIDTH // LANES
RW_CHUNK = 64
RW_DECAY_LORA = 64
RW_ICLR_LORA = 64
RW_VRES_LORA = 32
RW_GATE_LORA = 160
RW_GN_EPS = 64e-5
SSM_STATE = 128
SSM_GROUPS = 2
SSM_CHUNK = 128
SSM_CONV = 4
ATT_KV_HEADS = 2
WINDOW = 128
FFN_HIDDEN = 2816
NORM_EPS = 1e-6
VMEM_LIMIT = 56 * 1024 * 1024

C_GATE = 0
C_RKV = 3072
C_SSM = 4608
C_ATT = 6144
C_LGLV = 7168
C_LWLA = 7424
C_DT = 7552
IN_COLS_PAD = 7680


def _dot(a, b):
    return jnp.dot(a.astype(BF16), b.astype(BF16), preferred_element_type=F32)


def _dot_nt(a, b):
    return lax.dot_general(a.astype(BF16), b.astype(BF16), (((1,), (1,)), ((), ())),
                           preferred_element_type=F32)


def _dot_f32(a, b):
    return jnp.dot(a, b, precision=lax.Precision.HIGHEST, preferred_element_type=F32)


def _dot_split(a, b01):
    hi = a.astype(BF16)
    lo = (a - hi.astype(F32)).astype(BF16)
    return (jnp.dot(hi, b01, preferred_element_type=F32)
            + jnp.dot(lo, b01, preferred_element_type=F32))


def _sigmoid(x):
    return 1.0 / (1.0 + jnp.exp(-x))


def _softplus(x):
    return jnp.maximum(x, 0.0) + jnp.log(1.0 + jnp.exp(-jnp.abs(x)))


def _rms(x, w):
    return x * lax.rsqrt(jnp.mean(x * x, axis=-1, keepdims=True) + NORM_EPS) * w


def _shift_rows(carry8, p, j):
    ext = jnp.concatenate([carry8, p], axis=0)
    return pltpu.roll(ext, j, 0)[8:]


def _params(*sem):
    return pltpu.CompilerParams(dimension_semantics=sem, vmem_limit_bytes=VMEM_LIMIT)


def _resident(shape):
    nd = len(shape)
    return pl.BlockSpec(shape, lambda *_: (0,) * nd)


def _in_proj_kernel(x_ref, g_ref, w_ref, o_ref, *, chunk):
    xn = _rms(x_ref[...], g_ref[...]).astype(BF16)
    for c0 in range(0, IN_COLS_PAD, chunk):
        o_ref[:, c0:c0 + chunk] = jnp.dot(
            xn, w_ref[:, c0:c0 + chunk], preferred_element_type=F32).astype(BF16)


def in_proj(x, gamma, w, tm=512, chunk=512):
    n = x.shape[0]
    return pl.pallas_call(
        functools.partial(_in_proj_kernel, chunk=chunk),
        out_shape=jax.ShapeDtypeStruct((n, IN_COLS_PAD), BF16),
        grid=(n // tm,),
        in_specs=[pl.BlockSpec((tm, D_MODEL), lambda i: (i, 0)),
                  _resident((1, D_MODEL)),
                  _resident((D_MODEL, IN_COLS_PAD))],
        out_specs=pl.BlockSpec((tm, IN_COLS_PAD), lambda i: (i, 0)),
        compiler_params=_params("parallel"),
        name="in_proj",
    )(x, gamma, w)


def _tri_inverse(n_mat):
    size = n_mat.shape[0]
    row = lax.broadcasted_iota(jnp.int32, (size, size), 0)
    col = lax.broadcasted_iota(jnp.int32, (size, size), 1)
    eye = (row == col).astype(F32)

    def same_block(b):
        return (row // b) == (col // b)

    a0 = jnp.where(same_block(8), n_mat, 0.0)
    d = eye + a0
    p = _dot_f32(a0, a0)
    d = d + _dot_f32(d, p)
    p = _dot_f32(p, p)
    d = d + _dot_f32(d, p)
    b = 8
    while b < size:
        e = jnp.where(same_block(2 * b) & jnp.logical_not(same_block(b)), n_mat, 0.0)
        d = d + _dot_f32(d, _dot_f32(e, d))
        b *= 2
    return d


def _rwkv_kernel(*refs, has_vres, tb):
    if has_vres:
        (rkv_ref, lglv_ref, lwla_ref, vfirst_ref, mu_rkv, mu_lglv, mu_lwla, w0, w2p, a0, a2p,
         g2p, kkw, kaw, rkw, gnw, gnb, hsum, v0, v2p,
         y_ref,
         c_rkv, c_lglv, c_lwla, s_ref, r_s, k_s, v_s, kk_s, kka_s, lw_s, g_s, y_s) = refs
    else:
        (rkv_ref, lglv_ref, lwla_ref, mu_rkv, mu_lglv, mu_lwla, w0, w2p, a0, a2p,
         g2p, kkw, kaw, rkw, gnw, gnb, hsum,
         y_ref, vfirst_out,
         c_rkv, c_lglv, c_lwla, s_ref, r_s, k_s, v_s, kk_s, kka_s, lw_s, g_s, y_s) = refs

    @pl.when(pl.program_id(1) == 0)
    def _():
        c_rkv[...] = jnp.zeros_like(c_rkv)
        c_lglv[...] = jnp.zeros_like(c_lglv)
        c_lwla[...] = jnp.zeros_like(c_lwla)
        s_ref[...] = jnp.zeros_like(s_ref)

    def mix(p_ref, carry, mu):
        p = p_ref[...].astype(F32)
        prev = _shift_rows(carry[...], p, 1)
        carry[...] = p[tb - 8:]
        return p + (prev - p) * mu[...]

    rkv = mix(rkv_ref, c_rkv, mu_rkv)
    lglv = mix(lglv_ref, c_lglv, mu_lglv)
    lwla = mix(lwla_ref, c_lwla, mu_lwla)
    r = rkv[:, :WIDTH]
    k = rkv[:, WIDTH:2 * WIDTH]
    v = rkv[:, 2 * WIDTH:]

    w = -_softplus(-(w0[...] + _dot(jnp.tanh(lwla), w2p[...]))) - 0.5
    lw_s[...] = -jnp.exp(w)
    a = _sigmoid(a0[...] + _dot(lwla, a2p[...]))
    g_s[...] = _dot(_sigmoid(lglv), g2p[...])
    if has_vres:
        v = v + (vfirst_ref[...].astype(F32) - v) * _sigmoid(v0[...] + _dot(lglv, v2p[...]))
    else:
        vfirst_out[...] = v.astype(BF16)
    hs = hsum[...]
    kk = k * kkw[...]
    kk = kk / jnp.maximum(jnp.sqrt(_dot_split(kk * kk, hs)), 1e-12)
    k = k * (1.0 + (a - 1.0) * kaw[...])
    r_s[...] = r
    k_s[...] = k
    v_s[...] = v
    kk_s[...] = kk
    kka_s[...] = kk * a

    cl = RW_CHUNK
    row = lax.broadcasted_iota(jnp.int32, (cl, cl), 0)
    col = lax.broadcasted_iota(jnp.int32, (cl, cl), 1)
    tri_incl = (row >= col).astype(F32)
    row2 = lax.broadcasted_iota(jnp.int32, (2 * cl, LANES), 0)
    col2 = lax.broadcasted_iota(jnp.int32, (2 * cl, LANES), 1)
    t_idx = row2 % cl
    s_idx = col2 % cl
    m1_mask = (t_idx - s_idx) >= jnp.where(row2 < cl, 1, 0)
    lo = lax.broadcasted_iota(jnp.int32, (cl, LANES), 1) < HEAD_DIM
    rr = lax.broadcasted_iota(jnp.int32, (LANES, LANES), 0)
    cc = lax.broadcasted_iota(jnp.int32, (LANES, LANES), 1)
    same_head = (rr // HEAD_DIM) == (cc // HEAD_DIM)

    def chunk_body(ci, carry):
        rows = pl.ds(pl.multiple_of(ci * cl, cl), cl)
        lw = lw_s[rows, :]
        cum = _dot_f32(tri_incl, lw)
        mid = cum[cl // 2 - 1:cl // 2, :]
        cen = cum - mid
        e_pos = jnp.exp(cen)
        e_neg = jnp.exp(-cen)
        rt_all = r_s[rows, :] * e_pos
        at_all = -kk_s[rows, :] * jnp.exp(cen - lw)
        bt_all = kka_s[rows, :] * e_neg
        kt_all = k_s[rows, :] * e_neg
        v_all = v_s[rows, :]
        e_mid = jnp.exp(mid)
        e_end = jnp.exp(cen[cl - 1:cl, :])
        e_tot = jnp.exp(cum[cl - 1:cl, :])
        for p in range(N_PAIRS):
            ln = slice(p * LANES, (p + 1) * LANES)
            rt, at, bt, kt, vp = rt_all[:, ln], at_all[:, ln], bt_all[:, ln], kt_all[:, ln], v_all[:, ln]
            zero = jnp.zeros_like(rt)
            x_all = jnp.concatenate([jnp.where(lo, at, zero), jnp.where(lo, rt, zero),
                                     jnp.where(lo, zero, at), jnp.where(lo, zero, rt)], axis=0)
            z = jnp.concatenate([bt, kt], axis=0)
            s0 = s_ref[p]
            g0 = _dot_nt(x_all, s0 * e_mid[:, ln])
            m1 = _dot_nt(x_all, z)
            w_zero = jnp.concatenate([zero, vp], axis=0)
            u_heads = []
            m1_heads = []
            for h in range(2):
                base = h * 2 * cl
                m1h = jnp.where(m1_mask, m1[base:base + 2 * cl], 0.0)
                m1_heads.append(m1h)
                t_inv = _tri_inverse(m1h[:cl, :cl])
                rhs = g0[base:base + cl] + _dot(m1h[:cl], w_zero)
                u_heads.append(_dot_f32(t_inv, rhs))
            u_pair = jnp.where(lo, u_heads[0], u_heads[1])
            wmat = jnp.concatenate([u_pair, vp], axis=0)
            y_heads = []
            for h in range(2):
                base = h * 2 * cl
                y_heads.append(g0[base + cl:base + 2 * cl] + _dot(m1_heads[h][cl:], wmat))
            y_s[rows, ln] = jnp.where(lo, y_heads[0], y_heads[1])
            upd = _dot(wmat.T, z) * e_end[:, ln]
            s_ref[p] = jnp.where(same_head, s0 * e_tot[:, ln] + upd, 0.0)
        return carry

    lax.fori_loop(0, tb // cl, chunk_body, 0)

    y = y_s[...]
    inv_n = 1.0 / HEAD_DIM
    mean = _dot_split(y, hs) * inv_n
    yc = y - mean
    var = _dot_split(yc * yc, hs) * inv_n
    yn = yc * lax.rsqrt(var + RW_GN_EPS) * gnw[...] + gnb[...]
    bonus = _dot_split(r_s[...] * k_s[...] * rkw[...], hs) * v_s[...]
    y_ref[...] = ((yn + bonus) * g_s[...]).astype(BF16)


def rwkv_mixer(proj, vfirst, prm, batch, seq, tb=256):
    n = proj.shape[0]
    nt = seq // tb
    has_vres = vfirst is not None
    row = lambda b, t: b * nt + t
    in_specs = [pl.BlockSpec((tb, 3 * WIDTH), lambda b, t: (row(b, t), C_RKV // (3 * WIDTH))),
                pl.BlockSpec((tb, 256), lambda b, t: (row(b, t), C_LGLV // 256)),
                pl.BlockSpec((tb, 128), lambda b, t: (row(b, t), C_LWLA // 128))]
    args = [proj, proj, proj]
    if has_vres:
        in_specs.append(pl.BlockSpec((tb, WIDTH), lambda b, t: (row(b, t), 0)))
        args.append(vfirst)
    names = ["mu_rkv", "mu_lglv", "mu_lwla", "w0", "w2p", "a0", "a2p", "g2p", "kkw", "kaw", "rkw",
             "gnw", "gnb", "hsum"]
    if has_vres:
        names += ["v0", "v2p"]
    for nm in names:
        in_specs.append(_resident(prm[nm].shape))
        args.append(prm[nm])
    y_spec = pl.BlockSpec((tb, WIDTH), lambda b, t: (row(b, t), 0))
    y_shape = jax.ShapeDtypeStruct((n, WIDTH), BF16)
    if has_vres:
        out_shape, out_specs = y_shape, y_spec
    else:
        out_shape, out_specs = (y_shape, y_shape), (y_spec, y_spec)
    big = pltpu.VMEM((tb, WIDTH), F32)
    scratch = [pltpu.VMEM((8, 3 * WIDTH), F32), pltpu.VMEM((8, 256), F32), pltpu.VMEM((8, 128), F32),
               pltpu.VMEM((N_PAIRS, LANES, LANES), F32)] + [big] * 8
    return pl.pallas_call(
        functools.partial(_rwkv_kernel, has_vres=has_vres, tb=tb),
        out_shape=out_shape, grid=(batch, nt), in_specs=in_specs, out_specs=out_specs,
        scratch_shapes=scratch, compiler_params=_params("parallel", "arbitrary"),
        name="rwkv7_vres" if has_vres else "rwkv7",
    )(*args)


def _ssd_kernel(zx_ref, dt_ref, convw, convb, dtb, a_neg, dskip, normw, exp_ch, exp_ln,
                y_ref, c_conv, st_ref):
    q = SSM_CHUNK

    @pl.when(pl.program_id(1) == 0)
    def _():
        c_conv[...] = jnp.zeros_like(c_conv)
        st_ref[...] = jnp.zeros_like(st_ref)

    zx = zx_ref[...].astype(F32)
    z = zx[:, :WIDTH]
    xbc = zx[:, WIDTH:]
    carry = c_conv[...]
    conv = xbc * convw[SSM_CONV - 1:SSM_CONV, :] + convb[...]
    for j in range(1, SSM_CONV):
        conv = conv + _shift_rows(carry, xbc, j) * convw[SSM_CONV - 1 - j:SSM_CONV - j, :]
    c_conv[...] = xbc[q - 8:]
    xbc = conv * _sigmoid(conv)
    xs = xbc[:, :WIDTH]
    bm = xbc[:, WIDTH:WIDTH + SSM_GROUPS * SSM_STATE]
    cm = xbc[:, WIDTH + SSM_GROUPS * SSM_STATE:]

    dt = _softplus(dt_ref[...].astype(F32) + dtb[...])
    a = dt * a_neg[...]
    row = lax.broadcasted_iota(jnp.int32, (q, q), 0)
    col = lax.broadcasted_iota(jnp.int32, (q, q), 1)
    causal = row >= col
    acum = _dot_f32(causal.astype(F32), a)
    acum_t = acum.T
    ech = exp_ch[...]
    dt_x = _dot_split(dt, ech)
    acum_x = _dot_f32(acum, ech.astype(F32))
    atot_x = acum_x[q - 1:q, :]
    acum_l = _dot_f32(acum, exp_ln[...].astype(F32))
    xdt = xs * dt_x
    xdec = xdt * jnp.exp(atot_x - acum_x)
    e_in = jnp.exp(acum_x)
    e_tot = jnp.exp(atot_x)
    lo = lax.broadcasted_iota(jnp.int32, (q, LANES), 1) < HEAD_DIM
    per_g = N_HEADS // SSM_GROUPS
    gw = per_g * HEAD_DIM
    ys = []
    for g in range(SSM_GROUPS):
        bg = bm[:, g * SSM_STATE:(g + 1) * SSM_STATE]
        cg = cm[:, g * SSM_STATE:(g + 1) * SSM_STATE]
        cb = _dot_nt(cg, bg)
        gl = slice(g * gw, (g + 1) * gw)
        y_g = _dot(cg, st_ref[g]) * e_in[:, gl]
        st_ref[g] = st_ref[g] * e_tot[:, gl] + _dot(bg.T, xdec[:, gl])
        diag = []
        for pp in range(per_g // 2):
            xp = xdt[:, gl][:, pp * LANES:(pp + 1) * LANES]
            acc = None
            for hh in range(2):
                h = g * per_g + 2 * pp + hh
                diff = acum_l[:, h * LANES:(h + 1) * LANES] - acum_t[h:h + 1, :]
                lmat = jnp.where(causal, jnp.exp(jnp.where(causal, diff, 0.0)), 0.0)
                xh = jnp.where(lo, xp, 0.0) if hh == 0 else jnp.where(lo, 0.0, xp)
                t = _dot(cb * lmat, xh)
                acc = t if acc is None else acc + t
            diag.append(acc)
        ys.append(y_g + jnp.concatenate(diag, axis=1))
    y = jnp.concatenate(ys, axis=1) + xs * dskip[...]
    yz = y * (z * _sigmoid(z))
    outs = []
    for g in range(SSM_GROUPS):
        yg = yz[:, g * gw:(g + 1) * gw]
        outs.append(yg * lax.rsqrt(jnp.mean(yg * yg, axis=-1, keepdims=True) + NORM_EPS))
    y_ref[...] = (jnp.concatenate(outs, axis=1) * normw[...]).astype(BF16)


def ssd_mixer(proj, prm, batch, seq):
    n = proj.shape[0]
    q = SSM_CHUNK
    nt = seq // q
    row = lambda b, t: b * nt + t
    names = ["convw", "convb", "dtb", "a_neg", "dskip", "normw", "exp_ch", "exp_ln"]
    in_specs = [pl.BlockSpec((q, 3 * WIDTH), lambda b, t: (row(b, t), C_SSM // (3 * WIDTH))),
                pl.BlockSpec((q, 128), lambda b, t: (row(b, t), C_DT // 128))]
    in_specs += [_resident(prm[nm].shape) for nm in names]
    return pl.pallas_call(
        _ssd_kernel,
        out_shape=jax.ShapeDtypeStruct((n, WIDTH), BF16),
        grid=(batch, nt), in_specs=in_specs,
        out_specs=pl.BlockSpec((q, WIDTH), lambda b, t: (row(b, t), 0)),
        scratch_shapes=[pltpu.VMEM((8, WIDTH + 2 * SSM_GROUPS * SSM_STATE), F32),
                        pltpu.VMEM((SSM_GROUPS, SSM_STATE, WIDTH // SSM_GROUPS), F32)],
        compiler_params=_params("parallel", "arbitrary"),
        name="ssd",
    )(proj, proj, *[prm[nm] for nm in names])


def _swa_kernel(sink_ref, qkv_ref, y_ref, kv_prev):
    wdw = WINDOW

    @pl.when(pl.program_id(1) == 0)
    def _():
        kv_prev[...] = jnp.zeros_like(kv_prev)

    qkv = qkv_ref[...].astype(F32)
    q = qkv[:, :WIDTH]
    kv = qkv[:, WIDTH:]
    band = jnp.concatenate([kv_prev[...], kv], axis=0)
    kv_prev[...] = kv
    qi = lax.broadcasted_iota(jnp.int32, (wdw, 2 * wdw), 0)
    kj = lax.broadcasted_iota(jnp.int32, (wdw, 2 * wdw), 1)
    rel = qi + wdw - kj
    first = jnp.where(pl.program_id(1) > 0, 0, wdw)
    valid = (rel >= 0) & (rel < wdw) & (kj >= first)
    lo = lax.broadcasted_iota(jnp.int32, (wdw, LANES), 1) < HEAD_DIM
    lo2 = lax.broadcasted_iota(jnp.int32, (2 * wdw, LANES), 1) < HEAD_DIM
    rep = N_HEADS // ATT_KV_HEADS
    scale = HEAD_DIM ** -0.5
    outs = []
    for g in range(ATT_KV_HEADS):
        kb = band[:, g * LANES:(g + 1) * LANES]
        vb = band[:, (ATT_KV_HEADS + g) * LANES:(ATT_KV_HEADS + g + 1) * LANES]
        zero_v = jnp.zeros_like(vb)
        v_lo = jnp.where(lo2, vb, zero_v)
        v_hi = jnp.where(lo2, zero_v, vb)
        for pp in range(rep // 2):
            tile = g * (rep // 2) + pp
            qp = q[:, tile * LANES:(tile + 1) * LANES]
            zero_q = jnp.zeros_like(qp)
            acc = None
            for hh in range(2):
                h = 2 * tile + hh
                qh = jnp.where(lo, qp, zero_q) if hh == 0 else jnp.where(lo, zero_q, qp)
                s = _dot_nt(qh, kb) * scale
                s = jnp.where(valid, s, -1e30)
                sink = sink_ref[h]
                m = jnp.maximum(jnp.max(s, axis=-1, keepdims=True), sink)
                p = jnp.exp(s - m)
                den = jnp.sum(p, axis=-1, keepdims=True) + jnp.exp(sink - m)
                o = _dot(p / den, v_lo if hh == 0 else v_hi)
                acc = o if acc is None else acc + o
            outs.append(acc)
    y_ref[...] = jnp.concatenate(outs, axis=1).astype(BF16)


def swa_mixer(proj, sinks, batch, seq):
    n = proj.shape[0]
    nt = seq // WINDOW
    row = lambda b, t: b * nt + t
    return pl.pallas_call(
        _swa_kernel,
        out_shape=jax.ShapeDtypeStruct((n, WIDTH), BF16),
        grid=(batch, nt),
        in_specs=[pl.BlockSpec(memory_space=pltpu.SMEM),
                  pl.BlockSpec((WINDOW, 2 * WIDTH), lambda b, t: (row(b, t), C_ATT // (2 * WIDTH)))],
        out_specs=pl.BlockSpec((WINDOW, WIDTH), lambda b, t: (row(b, t), 0)),
        scratch_shapes=[pltpu.VMEM((WINDOW, WIDTH), F32)],
        compiler_params=_params("parallel", "arbitrary"),
        name="swa",
    )(sinks, proj)


def _merge_kernel(x_ref, gate_ref, yrw, yssm, yatt, gb, wrw, wssm, watt, wout, o_ref):
    merged = None
    for i, (y, w) in enumerate(((yrw, wrw), (yssm, wssm), (yatt, watt))):
        cols = slice(i * D_MODEL, (i + 1) * D_MODEL)
        gate = _sigmoid(gate_ref[:, cols].astype(F32) + gb[:, cols])
        t = gate * jnp.dot(y[...], w[...], preferred_element_type=F32)
        merged = t if merged is None else merged + t
    o_ref[...] = x_ref[...] + _dot(merged, wout[...])


def merge_out(x, proj, y_rw, y_ssm, y_att, prm, tm=512):
    n = x.shape[0]
    names = ["gate_b", "w_br_rw", "w_br_ssm", "w_br_att", "w_out"]
    rows = lambda width: pl.BlockSpec((tm, width), lambda i: (i, 0))
    return pl.pallas_call(
        _merge_kernel,
        out_shape=jax.ShapeDtypeStruct((n, D_MODEL), F32),
        grid=(n // tm,),
        in_specs=[rows(D_MODEL), rows(3 * D_MODEL), rows(WIDTH), rows(WIDTH), rows(WIDTH)]
        + [_resident(prm[nm].shape) for nm in names],
        out_specs=rows(D_MODEL),
        compiler_params=_params("parallel"),
        name="merge_out",
    )(x, proj, y_rw, y_ssm, y_att, *[prm[nm] for nm in names])


def _ffn_kernel(x_ref, gamma, wg, wu, wd, gfin, o_ref, act, *, chunk, final_norm):
    x = x_ref[...]
    xn = _rms(x, gamma[...]).astype(BF16)
    for c0 in range(0, FFN_HIDDEN, chunk):
        cs = slice(c0, c0 + chunk)
        gate = jnp.dot(xn, wg[:, cs], preferred_element_type=F32)
        up = jnp.dot(xn, wu[:, cs], preferred_element_type=F32)
        act[:, cs] = (gate * _sigmoid(gate) * up).astype(BF16)
    out = x + jnp.dot(act[...], wd[...], preferred_element_type=F32)
    if final_norm:
        out = _rms(out, gfin[...])
    o_ref[...] = out


def ffn(x, prm, gfin, final_norm, tm=512, chunk=256):
    n = x.shape[0]
    names = ["norm_ffn", "w_gate", "w_up", "w_down"]
    return pl.pallas_call(
        functools.partial(_ffn_kernel, chunk=chunk, final_norm=final_norm),
        out_shape=jax.ShapeDtypeStruct((n, D_MODEL), F32),
        grid=(n // tm,),
        in_specs=[pl.BlockSpec((tm, D_MODEL), lambda i: (i, 0))]
        + [_resident(prm[nm].shape) for nm in names] + [_resident(gfin.shape)],
        out_specs=pl.BlockSpec((tm, D_MODEL), lambda i: (i, 0)),
        scratch_shapes=[pltpu.VMEM((tm, FFN_HIDDEN), BF16)],
        compiler_params=_params("parallel"),
        name="ffn_final" if final_norm else "ffn",
    )(x, *[prm[nm] for nm in names], gfin)


def _pad_rows(w, rows, at=0):
    out = jnp.zeros((rows, w.shape[1]), w.dtype)
    return lax.dynamic_update_slice(out, w, (at, 0))


def _layer_params(l, p):
    rw_cols = 3 * WIDTH + RW_DECAY_LORA + RW_ICLR_LORA + RW_GATE_LORA
    ssm_cols = WIDTH + (WIDTH + 2 * SSM_GROUPS * SSM_STATE) + N_HEADS
    att_cols = WIDTH + 2 * ATT_KV_HEADS * HEAD_DIM
    o_ssm = rw_cols
    o_att = o_ssm + ssm_cols
    o_gate = o_att + att_cols
    w = p["w_in"][l]
    zeros = lambda c: jnp.zeros((D_MODEL, c), w.dtype)
    o_k = o_att + WIDTH
    o_v = o_k + ATT_KV_HEADS * HEAD_DIM
    dup = lambda o: [w[:, o + HEAD_DIM * (i // 2):o + HEAD_DIM * (i // 2 + 1)] for i in range(4)]
    lv = p["rw_vres_down"][l - 1] if l > 0 else zeros(RW_VRES_LORA)
    w_cat = jnp.concatenate(
        [w[:, o_gate:o_gate + 3 * D_MODEL],
         w[:, :3 * WIDTH],
         w[:, o_ssm:o_ssm + 3 * WIDTH],
         w[:, o_att:o_att + WIDTH]] + dup(o_k) + dup(o_v)
        + [w[:, 3 * WIDTH + 128:rw_cols], lv, zeros(256 - RW_GATE_LORA - RW_VRES_LORA),
           w[:, 3 * WIDTH:3 * WIDTH + 128],
           w[:, o_ssm + 3 * WIDTH:o_ssm + 3 * WIDTH + N_HEADS], zeros(128 - N_HEADS)], axis=1)
    assert w_cat.shape == (D_MODEL, IN_COLS_PAD)

    row = lambda v: v.reshape(1, -1).astype(F32)
    mu = p["rw_mu"][l]
    vmu = p["rw_vres_mu"][l - 1] if l > 0 else jnp.zeros((RW_VRES_LORA,), F32)
    head_of = jnp.arange(WIDTH) // HEAD_DIM
    prm = {
        "norm_mix": row(p["norm_mix"][l]),
        "w_cat": w_cat.astype(BF16),
        "mu_rkv": row(mu[:3 * WIDTH]),
        "mu_lwla": row(mu[3 * WIDTH:3 * WIDTH + 128]),
        "mu_lglv": row(jnp.concatenate([mu[3 * WIDTH + 128:], vmu,
                                        jnp.zeros((256 - RW_GATE_LORA - RW_VRES_LORA,), F32)])),
        "w0": row(p["rw_w0"][l]),
        "w2p": _pad_rows(p["rw_w2"][l], 128, 0).astype(BF16),
        "a0": row(p["rw_a0"][l]),
        "a2p": _pad_rows(p["rw_a2"][l], 128, RW_DECAY_LORA).astype(BF16),
        "g2p": _pad_rows(p["rw_g2"][l], 256, 0).astype(BF16),
        "kkw": row(p["rw_k_k"][l]), "kaw": row(p["rw_k_a"][l]), "rkw": row(p["rw_r_k"][l]),
        "gnw": row(p["rw_gn_w"][l]), "gnb": row(p["rw_gn_b"][l]),
        "hsum": (head_of[:, None] == head_of[None, :]).astype(BF16),
        "convw": p["ssm_conv_w"][l].astype(F32),
        "convb": row(p["ssm_conv_b"][l]),
        "dtb": row(jnp.pad(p["ssm_dt_bias"][l], (0, 128 - N_HEADS))),
        "a_neg": row(jnp.pad(-jnp.exp(p["ssm_a_log"][l].astype(F32)), (0, 128 - N_HEADS))),
        "dskip": row(jnp.repeat(p["ssm_d"][l], HEAD_DIM)),
        "normw": row(p["ssm_norm_w"][l]),
        "exp_ch": (jnp.arange(128)[:, None] == head_of[None, :]).astype(BF16),
        "exp_ln": (jnp.arange(128)[:, None] == (jnp.arange(N_HEADS * LANES) // LANES)[None, :]).astype(BF16),
        "sinks": p["att_sinks"][l].astype(F32),
        "gate_b": row(p["gate_b"][l]),
        "w_br_rw": p["w_br_rw"][l].astype(BF16),
        "w_br_ssm": p["w_br_ssm"][l].astype(BF16),
        "w_br_att": p["w_br_att"][l].astype(BF16),
        "w_out": p["w_out"][l].astype(BF16),
        "norm_ffn": row(p["norm_ffn"][l]),
        "w_gate": p["ffn_w_gu"][l][:, :FFN_HIDDEN].astype(BF16),
        "w_up": p["ffn_w_gu"][l][:, FFN_HIDDEN:].astype(BF16),
        "w_down": p["ffn_w_down"][l].astype(BF16),
    }
    if l > 0:
        prm["v0"] = row(p["rw_vres_v0"][l - 1])
        prm["v2p"] = _pad_rows(p["rw_vres_v2"][l - 1], 256, RW_GATE_LORA).astype(BF16)
    return prm


def kernel(x, norm_mix, w_in, rw_mu, rw_w0, rw_w2, rw_a0, rw_a2, rw_g2, rw_k_k, rw_k_a, rw_r_k, rw_gn_w, rw_gn_b, rw_vres_down, rw_vres_mu, rw_vres_v0, rw_vres_v2, ssm_conv_w, ssm_conv_b, ssm_dt_bias, ssm_a_log, ssm_d, ssm_norm_w, att_sinks, gate_b, w_br_rw, w_br_ssm, w_br_att, w_out, norm_ffn, ffn_w_gu, ffn_w_down, norm_final):
    p = dict(norm_mix=norm_mix, w_in=w_in, rw_mu=rw_mu, rw_w0=rw_w0, rw_w2=rw_w2, rw_a0=rw_a0,
             rw_a2=rw_a2, rw_g2=rw_g2, rw_k_k=rw_k_k, rw_k_a=rw_k_a,
             rw_r_k=rw_r_k.reshape(rw_r_k.shape[0], -1), rw_gn_w=rw_gn_w, rw_gn_b=rw_gn_b,
             rw_vres_down=rw_vres_down, rw_vres_mu=rw_vres_mu, rw_vres_v0=rw_vres_v0,
             rw_vres_v2=rw_vres_v2, ssm_conv_w=ssm_conv_w, ssm_conv_b=ssm_conv_b,
             ssm_dt_bias=ssm_dt_bias, ssm_a_log=ssm_a_log, ssm_d=ssm_d, ssm_norm_w=ssm_norm_w,
             att_sinks=att_sinks, gate_b=gate_b, w_br_rw=w_br_rw, w_br_ssm=w_br_ssm,
             w_br_att=w_br_att, w_out=w_out, norm_ffn=norm_ffn, ffn_w_gu=ffn_w_gu,
             ffn_w_down=ffn_w_down)
    batch, seq, _ = x.shape
    depth = w_in.shape[0]
    gfin = norm_final.reshape(1, -1).astype(F32)
    xf = x.reshape(batch * seq, D_MODEL)
    vfirst = None
    for l in range(depth):
        prm = _layer_params(l, p)
        proj = in_proj(xf, prm["norm_mix"], prm["w_cat"])
        if l == 0:
            y_rw, vfirst = rwkv_mixer(proj, None, prm, batch, seq)
        else:
            y_rw = rwkv_mixer(proj, vfirst, prm, batch, seq)
        y_ssm = ssd_mixer(proj, prm, batch, seq)
        y_att = swa_mixer(proj, prm["sinks"], batch, seq)
        xf = merge_out(xf, proj, y_rw, y_ssm, y_att, prm)
        xf = ffn(xf, prm, gfin, final_norm=(l == depth - 1))
    return xf.reshape(batch, seq, D_MODEL)
```

```python
import functools

import jax
import jax.numpy as jnp
from jax import lax
from jax.experimental import pallas as pl
from jax.experimental.pallas import tpu as pltpu

F32 = jnp.float32
BF16 = jnp.bfloat16

D_MODEL = 1024
HEAD_DIM = 64
N_HEADS = 8
WIDTH = N_HEADS * HEAD_DIM
LANES = 128
N_PAIRS = WIDTH // LANES
RW_CHUNK = 64
RW_DECAY_LORA = 64
RW_ICLR_LORA = 64
RW_VRES_LORA = 32
RW_GATE_LORA = 160
RW_GN_EPS = 64e-5
SSM_STATE = 128
SSM_GROUPS = 2
SSM_CHUNK = 128
SSM_CONV = 4
ATT_KV_HEADS = 2
WINDOW = 128
FFN_HIDDEN = 2816
NORM_EPS = 1e-6
VMEM_LIMIT = 56 * 1024 * 1024

C_GATE = 0
C_RKV = 3072
C_SSM = 4608
C_ATT = 6144
C_LGLV = 7168
C_LWLA = 7424
C_DT = 7552
IN_COLS_PAD = 7680


def _dot(a, b):
    return jnp.dot(a.astype(BF16), b.astype(BF16), preferred_element_type=F32)


def _dot_nt(a, b):
    return lax.dot_general(a.astype(BF16), b.astype(BF16), (((1,), (1,)), ((), ())),
                           preferred_element_type=F32)


def _split3(x):
    hi = x.astype(BF16)
    r1 = x - hi.astype(F32)
    mid = r1.astype(BF16)
    return hi, mid, (r1 - mid.astype(F32)).astype(BF16)


def _dot01_left(m01, x):
    return sum(jnp.dot(m01, part, preferred_element_type=F32) for part in _split3(x))


def _dot01_right(x, m01):
    return sum(jnp.dot(part, m01, preferred_element_type=F32) for part in _split3(x))


def _dot_split(a, b01):
    hi = a.astype(BF16)
    lo = (a - hi.astype(F32)).astype(BF16)
    return (jnp.dot(hi, b01, preferred_element_type=F32)
            + jnp.dot(lo, b01, preferred_element_type=F32))


def _sigmoid(x):
    return 1.0 / (1.0 + jnp.exp(-x))


def _softplus(x):
    return jnp.maximum(x, 0.0) + jnp.log(1.0 + jnp.exp(-jnp.abs(x)))


def _rms(x, w):
    return x * lax.rsqrt(jnp.mean(x * x, axis=-1, keepdims=True) + NORM_EPS) * w


def _shift_rows(carry8, p, j):
    ext = jnp.concatenate([carry8, p], axis=0)
    return pltpu.roll(ext, j, 0)[8:]


def _params(*sem):
    return pltpu.CompilerParams(dimension_semantics=sem, vmem_limit_bytes=VMEM_LIMIT)


def _resident(shape):
    nd = len(shape)
    return pl.BlockSpec(shape, lambda *_: (0,) * nd)


def _in_proj_kernel(x_ref, g_ref, w_ref, o_ref, *, chunk):
    xn = _rms(x_ref[...], g_ref[...]).astype(BF16)
    for c0 in range(0, IN_COLS_PAD, chunk):
        o_ref[:, c0:c0 + chunk] = jnp.dot(
            xn, w_ref[:, c0:c0 + chunk], preferred_element_type=F32).astype(BF16)


def in_proj(x, gamma, w, tm=512, chunk=512):
    n = x.shape[0]
    return pl.pallas_call(
        functools.partial(_in_proj_kernel, chunk=chunk),
        out_shape=jax.ShapeDtypeStruct((n, IN_COLS_PAD), BF16),
        grid=(n // tm,),
        in_specs=[pl.BlockSpec((tm, D_MODEL), lambda i: (i, 0)),
                  _resident((1, D_MODEL)),
                  _resident((D_MODEL, IN_COLS_PAD))],
        out_specs=pl.BlockSpec((tm, IN_COLS_PAD), lambda i: (i, 0)),
        compiler_params=_params("parallel"),
        name="in_proj",
    )(x, gamma, w)


def _split_bf16(x):
    hi = x.astype(BF16)
    return hi, (x - hi.astype(F32)).astype(BF16)


def _headwise_mm(a, b, bd, exact):
    m = a.shape[0]
    outs = []
    for p in range(a.shape[1] // LANES):
        ln = slice(p * LANES, (p + 1) * LANES)
        ap, bp = a[:, ln], b[:, ln]
        if exact:
            ah, al = _split_bf16(ap)
            bh, bl = _split_bf16(bp)
            wh = jnp.concatenate([bh, bh], axis=0) * bd
            wl = jnp.concatenate([bl, bl], axis=0) * bd
            r = jnp.dot(jnp.concatenate([ah, al], axis=0), wh, preferred_element_type=F32)
            outs.append(r[:m] + r[m:] + jnp.dot(ah, wl, preferred_element_type=F32))
        else:
            bh = bp.astype(BF16)
            outs.append(jnp.dot(ap.astype(BF16), jnp.concatenate([bh, bh], axis=0) * bd,
                                preferred_element_type=F32))
    return jnp.concatenate(outs, axis=1)


def _tri_inverse(n_cat, bd, exact_merge):
    size, width = n_cat.shape
    row = lax.broadcasted_iota(jnp.int32, (size, width), 0)
    col = lax.broadcasted_iota(jnp.int32, (size, width), 1) % size
    eye = jnp.where(row == col, 1.0, 0.0)

    def same_block(b):
        return (row // b) == (col // b)

    a0 = jnp.where(same_block(8), n_cat, 0.0)
    d = eye + a0
    p = _headwise_mm(a0, a0, bd, True)
    d = d + _headwise_mm(d, p, bd, True)
    p = _headwise_mm(p, p, bd, True)
    d = d + _headwise_mm(d, p, bd, True)
    b = 8
    while b < size:
        e = jnp.where(same_block(2 * b) & jnp.logical_not(same_block(b)), n_cat, 0.0)
        d = d + _headwise_mm(d, _headwise_mm(e, d, bd, exact_merge), bd, exact_merge)
        b *= 2
    return d


def _rwkv_kernel(*refs, has_vres, tb, group):
    if has_vres:
        (rkv_ref, lglv_ref, lwla_ref, vfirst_ref, mu_rkv, mu_lglv, mu_lwla, w0, w2p, a0, a2p,
         g2p, kkw, kaw, rkw, gnw, gnb, hsum, v0, v2p,
         y_ref,
         c_rkv, c_lglv, c_lwla, s_ref, phi_s, psi_s, etot_s,
         r_s, k_s, v_s, kk_s, kka_s, lw_s, g_s, y_s, rhat_s, y0_s) = refs
    else:
        (rkv_ref, lglv_ref, lwla_ref, mu_rkv, mu_lglv, mu_lwla, w0, w2p, a0, a2p,
         g2p, kkw, kaw, rkw, gnw, gnb, hsum,
         y_ref, vfirst_out,
         c_rkv, c_lglv, c_lwla, s_ref, phi_s, psi_s, etot_s,
         r_s, k_s, v_s, kk_s, kka_s, lw_s, g_s, y_s, rhat_s, y0_s) = refs

    @pl.when(pl.program_id(1) == 0)
    def _():
        c_rkv[...] = jnp.zeros_like(c_rkv)
        c_lglv[...] = jnp.zeros_like(c_lglv)
        c_lwla[...] = jnp.zeros_like(c_lwla)
        s_ref[...] = jnp.zeros_like(s_ref)

    def mix(p_ref, carry, mu):
        p = p_ref[...].astype(F32)
        prev = _shift_rows(carry[...], p, 1)
        carry[...] = p[tb - 8:]
        return p + (prev - p) * mu[...]

    rkv = mix(rkv_ref, c_rkv, mu_rkv)
    lglv = mix(lglv_ref, c_lglv, mu_lglv)
    lwla = mix(lwla_ref, c_lwla, mu_lwla)
    r = rkv[:, :WIDTH]
    k = rkv[:, WIDTH:2 * WIDTH]
    v = rkv[:, 2 * WIDTH:]

    w = -_softplus(-(w0[...] + _dot(jnp.tanh(lwla), w2p[...]))) - 0.5
    lw_s[...] = -jnp.exp(w)
    a = _sigmoid(a0[...] + _dot(lwla, a2p[...]))
    g_s[...] = _dot(_sigmoid(lglv), g2p[...])
    if has_vres:
        v = v + (vfirst_ref[...].astype(F32) - v) * _sigmoid(v0[...] + _dot(lglv, v2p[...]))
    else:
        vfirst_out[...] = v.astype(BF16)
    hs = hsum[...]
    kk = k * kkw[...]
    kk = kk / jnp.maximum(jnp.sqrt(_dot_split(kk * kk, hs)), 1e-12)
    k = k * (1.0 + (a - 1.0) * kaw[...])
    r_s[...] = r
    k_s[...] = k
    v_s[...] = v
    kk_s[...] = kk
    kka_s[...] = kk * a

    cl = RW_CHUNK
    n_chunks = tb // cl
    row = lax.broadcasted_iota(jnp.int32, (cl, cl), 0)
    col = lax.broadcasted_iota(jnp.int32, (cl, cl), 1)
    tri_incl = jnp.where(row >= col, 1.0, 0.0).astype(BF16)
    row2 = lax.broadcasted_iota(jnp.int32, (2 * cl, 2 * LANES), 0)
    col2 = lax.broadcasted_iota(jnp.int32, (2 * cl, 2 * LANES), 1)
    m_mask = (row2 % cl - col2 % cl) >= jnp.where(row2 < cl, 1, 0)
    rr = lax.broadcasted_iota(jnp.int32, (LANES, LANES), 0)
    cc = lax.broadcasted_iota(jnp.int32, (LANES, LANES), 1)
    same_head = (rr // HEAD_DIM) == (cc // HEAD_DIM)
    lane_lo = cc < HEAD_DIM
    bd = jnp.where(same_head, 1.0, 0.0).astype(BF16)
    same_head_w = jnp.concatenate([same_head, same_head], axis=1)
    same_head_t = jnp.concatenate([same_head, same_head], axis=0)
    zeros_half = jnp.zeros((cl, LANES), F32)

    def group_pre(gi, carry):
        rts, ats, vvs, zs, ms, e_mids, e_ends = [], [], [], [], [], [], []
        for g in range(group):
            ci = gi * group + g
            rows = pl.ds(pl.multiple_of(ci * cl, cl), cl)
            lw = lw_s[rows, :]
            cum = _dot01_left(tri_incl, lw)
            mid = cum[cl // 2 - 1:cl // 2, :]
            cen = cum - mid
            e_pos = jnp.exp(cen)
            e_neg = jnp.exp(-cen)
            rt = r_s[rows, :] * e_pos
            at = -kk_s[rows, :] * jnp.exp(cen - lw)
            bt = kka_s[rows, :] * e_neg
            kt = k_s[rows, :] * e_neg
            rts.append(rt)
            ats.append(at)
            vvs.append(v_s[rows, :])
            e_mids.append(jnp.exp(mid))
            e_ends.append(jnp.exp(cen[cl - 1:cl, :]))
            etot_s[ci] = jnp.broadcast_to(jnp.exp(cum[cl - 1:cl, :]), (8, WIDTH))
            for p in range(N_PAIRS):
                ln = slice(p * LANES, (p + 1) * LANES)
                z = jnp.concatenate([bt[:, ln], kt[:, ln]], axis=0)
                zt = z.T
                zr = pltpu.roll(zt, HEAD_DIM, 1)
                wgt = jnp.concatenate([jnp.where(lane_lo, zt, zr), jnp.where(lane_lo, zr, zt)], axis=1)
                wgt = jnp.where(same_head_w, wgt, 0.0)
                m = _dot(jnp.concatenate([at[:, ln], rt[:, ln]], axis=0), wgt)
                zs.append(z)
                ms.append(jnp.where(m_mask, m, 0.0))
        rt = jnp.concatenate(rts, axis=1)
        at = jnp.concatenate(ats, axis=1)
        vv = jnp.concatenate(vvs, axis=1)
        e_mid = jnp.concatenate(e_mids, axis=1)
        a_ab = jnp.concatenate([m[:cl, :LANES] for m in ms], axis=1)
        a_kr = jnp.concatenate(
            [jnp.concatenate([m[:cl, LANES:] for m in ms], axis=1),
             jnp.concatenate([m[cl:, LANES:] for m in ms], axis=1)], axis=0)
        a_rb = jnp.concatenate([m[cl:, :LANES] for m in ms], axis=1)
        t_inv = _tri_inverse(a_ab, bd, True)
        akv = _headwise_mm(a_kr, vv, bd, False)
        at_t = _headwise_mm(t_inv, at, bd, True)
        u0 = _headwise_mm(t_inv, akv[:cl], bd, True)
        rhat = (rt + _headwise_mm(a_rb, at_t, bd, False)) * e_mid
        y0 = _headwise_mm(a_rb, u0, bd, False) + akv[cl:]
        at_m = at_t * e_mid
        for g in range(group):
            ci = gi * group + g
            rows = pl.ds(pl.multiple_of(ci * cl, cl), cl)
            rhat_s[rows, :] = rhat[:, g * WIDTH:(g + 1) * WIDTH]
            y0_s[rows, :] = y0[:, g * WIDTH:(g + 1) * WIDTH]
            for p in range(N_PAIRS):
                ln = slice(g * WIDTH + p * LANES, g * WIDTH + (p + 1) * LANES)
                lhs = jnp.concatenate(
                    [jnp.concatenate([at_m[:, ln], zeros_half], axis=0),
                     jnp.concatenate([u0[:, ln], vv[:, ln]], axis=0)], axis=1)
                pp = _dot(lhs.T, zs[g * N_PAIRS + p]) * e_ends[g][:, p * LANES:(p + 1) * LANES]
                pp = jnp.where(same_head_t, pp, 0.0)
                phi_s[ci, p] = pp[:LANES]
                psi_s[ci, p] = pp[LANES:]
        return carry

    if n_chunks == group:
        group_pre(0, 0)
    else:
        lax.fori_loop(0, n_chunks // group, group_pre, 0)

    def chunk_seq(ci, carry):
        rows = pl.ds(pl.multiple_of(ci * cl, cl), cl)
        rh = rhat_s[rows, :]
        y0 = y0_s[rows, :]
        et = etot_s[ci][0:1]
        for p in range(N_PAIRS):
            ln = slice(p * LANES, (p + 1) * LANES)
            s0 = s_ref[p]
            y_s[rows, ln] = _dot_nt(rh[:, ln], s0) + y0[:, ln]
            s_ref[p] = s0 * et[:, ln] + _dot(s0, phi_s[ci, p]) + psi_s[ci, p]
        return carry

    lax.fori_loop(0, n_chunks, chunk_seq, 0)

    y = y_s[...]
    inv_n = 1.0 / HEAD_DIM
    mean = _dot_split(y, hs) * inv_n
    yc = y - mean
    var = _dot_split(yc * yc, hs) * inv_n
    yn = yc * lax.rsqrt(var + RW_GN_EPS) * gnw[...] + gnb[...]
    bonus = _dot_split(r_s[...] * k_s[...] * rkw[...], hs) * v_s[...]
    y_ref[...] = ((yn + bonus) * g_s[...]).astype(BF16)


def rwkv_mixer(proj, vfirst, prm, batch, seq, tb=256, group=4):
    n = proj.shape[0]
    nt = seq // tb
    has_vres = vfirst is not None
    row = lambda b, t: b * nt + t
    in_specs = [pl.BlockSpec((tb, 3 * WIDTH), lambda b, t: (row(b, t), C_RKV // (3 * WIDTH))),
                pl.BlockSpec((tb, 256), lambda b, t: (row(b, t), C_LGLV // 256)),
                pl.BlockSpec((tb, 128), lambda b, t: (row(b, t), C_LWLA // 128))]
    args = [proj, proj, proj]
    if has_vres:
        in_specs.append(pl.BlockSpec((tb, WIDTH), lambda b, t: (row(b, t), 0)))
        args.append(vfirst)
    names = ["mu_rkv", "mu_lglv", "mu_lwla", "w0", "w2p", "a0", "a2p", "g2p", "kkw", "kaw", "rkw",
             "gnw", "gnb", "hsum"]
    if has_vres:
        names += ["v0", "v2p"]
    for nm in names:
        in_specs.append(_resident(prm[nm].shape))
        args.append(prm[nm])
    y_spec = pl.BlockSpec((tb, WIDTH), lambda b, t: (row(b, t), 0))
    y_shape = jax.ShapeDtypeStruct((n, WIDTH), BF16)
    if has_vres:
        out_shape, out_specs = y_shape, y_spec
    else:
        out_shape, out_specs = (y_shape, y_shape), (y_spec, y_spec)
    big = pltpu.VMEM((tb, WIDTH), F32)
    nc = tb // RW_CHUNK
    pair_mats = pltpu.VMEM((nc, N_PAIRS, LANES, LANES), F32)
    scratch = [pltpu.VMEM((8, 3 * WIDTH), F32), pltpu.VMEM((8, 256), F32), pltpu.VMEM((8, 128), F32),
               pltpu.VMEM((N_PAIRS, LANES, LANES), F32), pair_mats, pair_mats,
               pltpu.VMEM((nc, 8, WIDTH), F32)] + [big] * 10
    return pl.pallas_call(
        functools.partial(_rwkv_kernel, has_vres=has_vres, tb=tb, group=group),
        out_shape=out_shape, grid=(batch, nt), in_specs=in_specs, out_specs=out_specs,
        scratch_shapes=scratch, compiler_params=_params("parallel", "arbitrary"),
        name="rwkv7_vres" if has_vres else "rwkv7",
    )(*args)


def _ssd_kernel(zx_ref, dt_ref, convw, convb, dtb, a_neg, dskip, normw, exp_ch, exp_ln,
                y_ref, c_conv, st_ref):
    q = SSM_CHUNK

    @pl.when(pl.program_id(1) == 0)
    def _():
        c_conv[...] = jnp.zeros_like(c_conv)
        st_ref[...] = jnp.zeros_like(st_ref)

    zx = zx_ref[...].astype(F32)
    z = zx[:, :WIDTH]
    xbc = zx[:, WIDTH:]
    carry = c_conv[...]
    conv = xbc * convw[SSM_CONV - 1:SSM_CONV, :] + convb[...]
    for j in range(1, SSM_CONV):
        conv = conv + _shift_rows(carry, xbc, j) * convw[SSM_CONV - 1 - j:SSM_CONV - j, :]
    c_conv[...] = xbc[q - 8:]
    xbc = conv * _sigmoid(conv)
    xs = xbc[:, :WIDTH]
    bm = xbc[:, WIDTH:WIDTH + SSM_GROUPS * SSM_STATE]
    cm = xbc[:, WIDTH + SSM_GROUPS * SSM_STATE:]

    dt = _softplus(dt_ref[...].astype(F32) + dtb[...])
    a = dt * a_neg[...]
    row = lax.broadcasted_iota(jnp.int32, (q, q), 0)
    col = lax.broadcasted_iota(jnp.int32, (q, q), 1)
    causal = row >= col
    acum = _dot01_left(jnp.where(causal, 1.0, 0.0).astype(BF16), a)
    acum_t = acum.T
    ech = exp_ch[...]
    dt_x = _dot_split(dt, ech)
    acum_x = _dot01_right(acum, ech)
    atot_x = acum_x[q - 1:q, :]
    acum_l = _dot01_right(acum, exp_ln[...])
    xdt = xs * dt_x
    xdec = xdt * jnp.exp(atot_x - acum_x)
    e_in = jnp.exp(acum_x)
    e_tot = jnp.exp(atot_x)
    lo = lax.broadcasted_iota(jnp.int32, (q, LANES), 1) < HEAD_DIM
    per_g = N_HEADS // SSM_GROUPS
    gw = per_g * HEAD_DIM
    ys = []
    for g in range(SSM_GROUPS):
        bg = bm[:, g * SSM_STATE:(g + 1) * SSM_STATE]
        cg = cm[:, g * SSM_STATE:(g + 1) * SSM_STATE]
        cb = _dot_nt(cg, bg)
        gl = slice(g * gw, (g + 1) * gw)
        y_g = _dot(cg, st_ref[g]) * e_in[:, gl]
        st_ref[g] = st_ref[g] * e_tot[:, gl] + _dot(bg.T, xdec[:, gl])
        diag = []
        for pp in range(per_g // 2):
            xp = xdt[:, gl][:, pp * LANES:(pp + 1) * LANES]
            acc = None
            for hh in range(2):
                h = g * per_g + 2 * pp + hh
                diff = acum_l[:, h * LANES:(h + 1) * LANES] - acum_t[h:h + 1, :]
                lmat = jnp.where(causal, jnp.exp(jnp.where(causal, diff, 0.0)), 0.0)
                xh = jnp.where(lo, xp, 0.0) if hh == 0 else jnp.where(lo, 0.0, xp)
                t = _dot(cb * lmat, xh)
                acc = t if acc is None else acc + t
            diag.append(acc)
        ys.append(y_g + jnp.concatenate(diag, axis=1))
    y = jnp.concatenate(ys, axis=1) + xs * dskip[...]
    yz = y * (z * _sigmoid(z))
    outs = []
    for g in range(SSM_GROUPS):
        yg = yz[:, g * gw:(g + 1) * gw]
        outs.append(yg * lax.rsqrt(jnp.mean(yg * yg, axis=-1, keepdims=True) + NORM_EPS))
    y_ref[...] = (jnp.concatenate(outs, axis=1) * normw[...]).astype(BF16)


def ssd_mixer(proj, prm, batch, seq):
    n = proj.shape[0]
    q = SSM_CHUNK
    nt = seq // q
    row = lambda b, t: b * nt + t
    names = ["convw", "convb", "dtb", "a_neg", "dskip", "normw", "exp_ch", "exp_ln"]
    in_specs = [pl.BlockSpec((q, 3 * WIDTH), lambda b, t: (row(b, t), C_SSM // (3 * WIDTH))),
                pl.BlockSpec((q, 128), lambda b, t: (row(b, t), C_DT // 128))]
    in_specs += [_resident(prm[nm].shape) for nm in names]
    return pl.pallas_call(
        _ssd_kernel,
        out_shape=jax.ShapeDtypeStruct((n, WIDTH), BF16),
        grid=(batch, nt), in_specs=in_specs,
        out_specs=pl.BlockSpec((q, WIDTH), lambda b, t: (row(b, t), 0)),
        scratch_shapes=[pltpu.VMEM((8, WIDTH + 2 * SSM_GROUPS * SSM_STATE), F32),
                        pltpu.VMEM((SSM_GROUPS, SSM_STATE, WIDTH // SSM_GROUPS), F32)],
        compiler_params=_params("parallel", "arbitrary"),
        name="ssd",
    )(proj, proj, *[prm[nm] for nm in names])


def _swa_kernel(sink_ref, qkv_ref, y_ref, kv_prev):
    wdw = WINDOW

    @pl.when(pl.program_id(1) == 0)
    def _():
        kv_prev[...] = jnp.zeros_like(kv_prev)

    qkv = qkv_ref[...].astype(F32)
    q = qkv[:, :WIDTH]
    kv = qkv[:, WIDTH:]
    band = jnp.concatenate([kv_prev[...], kv], axis=0)
    kv_prev[...] = kv
    qi = lax.broadcasted_iota(jnp.int32, (wdw, 2 * wdw), 0)
    kj = lax.broadcasted_iota(jnp.int32, (wdw, 2 * wdw), 1)
    rel = qi + wdw - kj
    first = jnp.where(pl.program_id(1) > 0, 0, wdw)
    valid = (rel >= 0) & (rel < wdw) & (kj >= first)
    lo = lax.broadcasted_iota(jnp.int32, (wdw, LANES), 1) < HEAD_DIM
    lo2 = lax.broadcasted_iota(jnp.int32, (2 * wdw, LANES), 1) < HEAD_DIM
    rep = N_HEADS // ATT_KV_HEADS
    scale = HEAD_DIM ** -0.5
    scores = []
    for h in range(N_HEADS):
        g = h // rep
        qp = q[:, (h // 2) * LANES:(h // 2 + 1) * LANES]
        qh = jnp.where(lo, qp, 0.0) if h % 2 == 0 else jnp.where(lo, 0.0, qp)
        s = _dot_nt(qh, band[:, g * LANES:(g + 1) * LANES]) * scale
        scores.append(jnp.where(valid, s, -1e30))
    probs = []
    for h in range(N_HEADS):
        sink = sink_ref[h]
        m = jnp.maximum(jnp.max(scores[h], axis=-1, keepdims=True), sink)
        p = jnp.exp(scores[h] - m)
        den = jnp.sum(p, axis=-1, keepdims=True) + jnp.exp(sink - m)
        probs.append(p / den)
    outs = []
    for tile in range(N_HEADS // 2):
        g = (2 * tile) // rep
        vb = band[:, (ATT_KV_HEADS + g) * LANES:(ATT_KV_HEADS + g + 1) * LANES]
        outs.append(_dot(probs[2 * tile], jnp.where(lo2, vb, 0.0))
                    + _dot(probs[2 * tile + 1], jnp.where(lo2, 0.0, vb)))
    y_ref[...] = jnp.concatenate(outs, axis=1).astype(BF16)


def swa_mixer(proj, sinks, batch, seq):
    n = proj.shape[0]
    nt = seq // WINDOW
    row = lambda b, t: b * nt + t
    return pl.pallas_call(
        _swa_kernel,
        out_shape=jax.ShapeDtypeStruct((n, WIDTH), BF16),
        grid=(batch, nt),
        in_specs=[pl.BlockSpec(memory_space=pltpu.SMEM),
                  pl.BlockSpec((WINDOW, 2 * WIDTH), lambda b, t: (row(b, t), C_ATT // (2 * WIDTH)))],
        out_specs=pl.BlockSpec((WINDOW, WIDTH), lambda b, t: (row(b, t), 0)),
        scratch_shapes=[pltpu.VMEM((WINDOW, WIDTH), F32)],
        compiler_params=_params("parallel", "arbitrary"),
        name="swa",
    )(sinks, proj)


def _merge_kernel(x_ref, gate_ref, yrw, yssm, yatt, gb, wrw, wssm, watt, wout, o_ref):
    merged = None
    for i, (y, w) in enumerate(((yrw, wrw), (yssm, wssm), (yatt, watt))):
        cols = slice(i * D_MODEL, (i + 1) * D_MODEL)
        gate = _sigmoid(gate_ref[:, cols].astype(F32) + gb[:, cols])
        t = gate * jnp.dot(y[...], w[...], preferred_element_type=F32)
        merged = t if merged is None else merged + t
    o_ref[...] = x_ref[...] + _dot(merged, wout[...])


def merge_out(x, proj, y_rw, y_ssm, y_att, prm, tm=512):
    n = x.shape[0]
    names = ["gate_b", "w_br_rw", "w_br_ssm", "w_br_att", "w_out"]
    rows = lambda width: pl.BlockSpec((tm, width), lambda i: (i, 0))
    return pl.pallas_call(
        _merge_kernel,
        out_shape=jax.ShapeDtypeStruct((n, D_MODEL), F32),
        grid=(n // tm,),
        in_specs=[rows(D_MODEL), rows(3 * D_MODEL), rows(WIDTH), rows(WIDTH), rows(WIDTH)]
        + [_resident(prm[nm].shape) for nm in names],
        out_specs=rows(D_MODEL),
        compiler_params=_params("parallel"),
        name="merge_out",
    )(x, proj, y_rw, y_ssm, y_att, *[prm[nm] for nm in names])


def _ffn_kernel(x_ref, gamma, wg, wu, wd, gfin, o_ref, act, *, chunk, final_norm):
    x = x_ref[...]
    xn = _rms(x, gamma[...]).astype(BF16)
    for c0 in range(0, FFN_HIDDEN, chunk):
        cs = slice(c0, c0 + chunk)
        gate = jnp.dot(xn, wg[:, cs], preferred_element_type=F32)
        up = jnp.dot(xn, wu[:, cs], preferred_element_type=F32)
        act[:, cs] = (gate * _sigmoid(gate) * up).astype(BF16)
    out = x + jnp.dot(act[...], wd[...], preferred_element_type=F32)
    if final_norm:
        out = _rms(out, gfin[...])
    o_ref[...] = out


def ffn(x, prm, gfin, final_norm, tm=512, chunk=256):
    n = x.shape[0]
    names = ["norm_ffn", "w_gate", "w_up", "w_down"]
    return pl.pallas_call(
        functools.partial(_ffn_kernel, chunk=chunk, final_norm=final_norm),
        out_shape=jax.ShapeDtypeStruct((n, D_MODEL), F32),
        grid=(n // tm,),
        in_specs=[pl.BlockSpec((tm, D_MODEL), lambda i: (i, 0))]
        + [_resident(prm[nm].shape) for nm in names] + [_resident(gfin.shape)],
        out_specs=pl.BlockSpec((tm, D_MODEL), lambda i: (i, 0)),
        scratch_shapes=[pltpu.VMEM((tm, FFN_HIDDEN), BF16)],
        compiler_params=_params("parallel"),
        name="ffn_final" if final_norm else "ffn",
    )(x, *[prm[nm] for nm in names], gfin)


def _pad_rows(w, rows, at=0):
    out = jnp.zeros((rows, w.shape[1]), w.dtype)
    return lax.dynamic_update_slice(out, w, (at, 0))


def _layer_params(l, p):
    rw_cols = 3 * WIDTH + RW_DECAY_LORA + RW_ICLR_LORA + RW_GATE_LORA
    ssm_cols = WIDTH + (WIDTH + 2 * SSM_GROUPS * SSM_STATE) + N_HEADS
    att_cols = WIDTH + 2 * ATT_KV_HEADS * HEAD_DIM
    o_ssm = rw_cols
    o_att = o_ssm + ssm_cols
    o_gate = o_att + att_cols
    w = p["w_in"][l]
    zeros = lambda c: jnp.zeros((D_MODEL, c), w.dtype)
    o_k = o_att + WIDTH
    o_v = o_k + ATT_KV_HEADS * HEAD_DIM
    dup = lambda o: [w[:, o + HEAD_DIM * (i // 2):o + HEAD_DIM * (i // 2 + 1)] for i in range(4)]
    lv = p["rw_vres_down"][l - 1] if l > 0 else zeros(RW_VRES_LORA)
    w_cat = jnp.concatenate(
        [w[:, o_gate:o_gate + 3 * D_MODEL],
         w[:, :3 * WIDTH],
         w[:, o_ssm:o_ssm + 3 * WIDTH],
         w[:, o_att:o_att + WIDTH]] + dup(o_k) + dup(o_v)
        + [w[:, 3 * WIDTH + 128:rw_cols], lv, zeros(256 - RW_GATE_LORA - RW_VRES_LORA),
           w[:, 3 * WIDTH:3 * WIDTH + 128],
           w[:, o_ssm + 3 * WIDTH:o_ssm + 3 * WIDTH + N_HEADS], zeros(128 - N_HEADS)], axis=1)
    assert w_cat.shape == (D_MODEL, IN_COLS_PAD)

    row = lambda v: v.reshape(1, -1).astype(F32)
    mu = p["rw_mu"][l]
    vmu = p["rw_vres_mu"][l - 1] if l > 0 else jnp.zeros((RW_VRES_LORA,), F32)
    head_of = jnp.arange(WIDTH) // HEAD_DIM
    prm = {
        "norm_mix": row(p["norm_mix"][l]),
        "w_cat": w_cat.astype(BF16),
        "mu_rkv": row(mu[:3 * WIDTH]),
        "mu_lwla": row(mu[3 * WIDTH:3 * WIDTH + 128]),
        "mu_lglv": row(jnp.concatenate([mu[3 * WIDTH + 128:], vmu,
                                        jnp.zeros((256 - RW_GATE_LORA - RW_VRES_LORA,), F32)])),
        "w0": row(p["rw_w0"][l]),
        "w2p": _pad_rows(p["rw_w2"][l], 128, 0).astype(BF16),
        "a0": row(p["rw_a0"][l]),
        "a2p": _pad_rows(p["rw_a2"][l], 128, RW_DECAY_LORA).astype(BF16),
        "g2p": _pad_rows(p["rw_g2"][l], 256, 0).astype(BF16),
        "kkw": row(p["rw_k_k"][l]), "kaw": row(p["rw_k_a"][l]), "rkw": row(p["rw_r_k"][l]),
        "gnw": row(p["rw_gn_w"][l]), "gnb": row(p["rw_gn_b"][l]),
        "hsum": (head_of[:, None] == head_of[None, :]).astype(BF16),
        "convw": p["ssm_conv_w"][l].astype(F32),
        "convb": row(p["ssm_conv_b"][l]),
        "dtb": row(jnp.pad(p["ssm_dt_bias"][l], (0, 128 - N_HEADS))),
        "a_neg": row(jnp.pad(-jnp.exp(p["ssm_a_log"][l].astype(F32)), (0, 128 - N_HEADS))),
        "dskip": row(jnp.repeat(p["ssm_d"][l], HEAD_DIM)),
        "normw": row(p["ssm_norm_w"][l]),
        "exp_ch": (jnp.arange(128)[:, None] == head_of[None, :]).astype(BF16),
        "exp_ln": (jnp.arange(128)[:, None] == (jnp.arange(N_HEADS * LANES) // LANES)[None, :]).astype(BF16),
        "sinks": p["att_sinks"][l].astype(F32),
        "gate_b": row(p["gate_b"][l]),
        "w_br_rw": p["w_br_rw"][l].astype(BF16),
        "w_br_ssm": p["w_br_ssm"][l].astype(BF16),
        "w_br_att": p["w_br_att"][l].astype(BF16),
        "w_out": p["w_out"][l].astype(BF16),
        "norm_ffn": row(p["norm_ffn"][l]),
        "w_gate": p["ffn_w_gu"][l][:, :FFN_HIDDEN].astype(BF16),
        "w_up": p["ffn_w_gu"][l][:, FFN_HIDDEN:].astype(BF16),
        "w_down": p["ffn_w_down"][l].astype(BF16),
    }
    if l > 0:
        prm["v0"] = row(p["rw_vres_v0"][l - 1])
        prm["v2p"] = _pad_rows(p["rw_vres_v2"][l - 1], 256, RW_GATE_LORA).astype(BF16)
    return prm


def kernel(x, norm_mix, w_in, rw_mu, rw_w0, rw_w2, rw_a0, rw_a2, rw_g2, rw_k_k, rw_k_a, rw_r_k, rw_gn_w, rw_gn_b, rw_vres_down, rw_vres_mu, rw_vres_v0, rw_vres_v2, ssm_conv_w, ssm_conv_b, ssm_dt_bias, ssm_a_log, ssm_d, ssm_norm_w, att_sinks, gate_b, w_br_rw, w_br_ssm, w_br_att, w_out, norm_ffn, ffn_w_gu, ffn_w_down, norm_final):
    p = dict(norm_mix=norm_mix, w_in=w_in, rw_mu=rw_mu, rw_w0=rw_w0, rw_w2=rw_w2, rw_a0=rw_a0,
             rw_a2=rw_a2, rw_g2=rw_g2, rw_k_k=rw_k_k, rw_k_a=rw_k_a,
             rw_r_k=rw_r_k.reshape(rw_r_k.shape[0], -1), rw_gn_w=rw_gn_w, rw_gn_b=rw_gn_b,
             rw_vres_down=rw_vres_down, rw_vres_mu=rw_vres_mu, rw_vres_v0=rw_vres_v0,
             rw_vres_v2=rw_vres_v2, ssm_conv_w=ssm_conv_w, ssm_conv_b=ssm_conv_b,
             ssm_dt_bias=ssm_dt_bias, ssm_a_log=ssm_a_log, ssm_d=ssm_d, ssm_norm_w=ssm_norm_w,
             att_sinks=att_sinks, gate_b=gate_b, w_br_rw=w_br_rw, w_br_ssm=w_br_ssm,
             w_br_att=w_br_att, w_out=w_out, norm_ffn=norm_ffn, ffn_w_gu=ffn_w_gu,
             ffn_w_down=ffn_w_down)
    batch, seq, _ = x.shape
    depth = w_in.shape[0]
    gfin = norm_final.reshape(1, -1).astype(F32)
    xf = x.reshape(batch * seq, D_MODEL)
    vfirst = None
    for l in range(depth):
        prm = _layer_params(l, p)
        proj = in_proj(xf, prm["norm_mix"], prm["w_cat"])
        if l == 0:
            y_rw, vfirst = rwkv_mixer(proj, None, prm, batch, seq)
        else:
            y_rw = rwkv_mixer(proj, vfirst, prm, batch, seq)
        y_ssm = ssd_mixer(proj, prm, batch, seq)
        y_att = swa_mixer(proj, prm["sinks"], batch, seq)
        xf = merge_out(xf, proj, y_rw, y_ssm, y_att, prm)
        xf = ffn(xf, prm, gfin, final_norm=(l == depth - 1))
    return xf.reshape(batch, seq, D_MODEL)
```

```python
import functools

import jax
import jax.numpy as jnp
from jax import lax
from jax.experimental import pallas as pl
from jax.experimental.pallas import tpu as pltpu

F32 = jnp.float32
BF16 = jnp.bfloat16

D_MODEL = 1024
HEAD_DIM = 64
N_HEADS = 8
WIDTH = N_HEADS * HEAD_DIM
LANES = 128
N_PAIRS = WIDTH // LANES
RW_CHUNK = 64
RW_DECAY_LORA = 64
RW_ICLR_LORA = 64
RW_VRES_LORA = 32
RW_GATE_LORA = 160
RW_GN_EPS = 64e-5
SSM_STATE = 128
SSM_GROUPS = 2
SSM_CHUNK = 128
SSM_CONV = 4
ATT_KV_HEADS = 2
WINDOW = 128
FFN_HIDDEN = 2816
NORM_EPS = 1e-6
VMEM_LIMIT = 56 * 1024 * 1024

C_GATE = 0
C_RKV = 3072
C_SSM = 4608
C_ATT = 6144
C_LGLV = 7168
C_LWLA = 7424
C_DT = 7552
IN_COLS_PAD = 7680


def _dot(a, b):
    return jnp.dot(a.astype(BF16), b.astype(BF16), preferred_element_type=F32)


def _dot_nt(a, b):
    return lax.dot_general(a.astype(BF16), b.astype(BF16), (((1,), (1,)), ((), ())),
                           preferred_element_type=F32)


def _split3(x):
    hi = x.astype(BF16)
    r1 = x - hi.astype(F32)
    mid = r1.astype(BF16)
    return hi, mid, (r1 - mid.astype(F32)).astype(BF16)


def _dot01_left(m01, x):
    return sum(jnp.dot(m01, part, preferred_element_type=F32) for part in _split3(x))


def _dot01_right(x, m01):
    return sum(jnp.dot(part, m01, preferred_element_type=F32) for part in _split3(x))


def _dot_split(a, b01):
    hi = a.astype(BF16)
    lo = (a - hi.astype(F32)).astype(BF16)
    return (jnp.dot(hi, b01, preferred_element_type=F32)
            + jnp.dot(lo, b01, preferred_element_type=F32))


def _sigmoid(x):
    return 1.0 / (1.0 + jnp.exp(-x))


def _softplus(x):
    return jnp.maximum(x, 0.0) + jnp.log(1.0 + jnp.exp(-jnp.abs(x)))


def _rms(x, w):
    return x * lax.rsqrt(jnp.mean(x * x, axis=-1, keepdims=True) + NORM_EPS) * w


def _shift_rows(carry8, p, j):
    ext = jnp.concatenate([carry8, p], axis=0)
    return pltpu.roll(ext, j, 0)[8:]


def _params(*sem):
    return pltpu.CompilerParams(dimension_semantics=sem, vmem_limit_bytes=VMEM_LIMIT)


_SHARED = ("hsum", "exp_ch", "exp_ln")
_FROM_SECOND_LAYER = ("v0", "v2p")


def _layer_specs(prm, names, l):
    specs, args = [], []
    for nm in names:
        arr = prm[nm]
        idx = 0 if nm in _SHARED else (l - 1 if nm in _FROM_SECOND_LAYER else l)
        nd = arr.ndim
        specs.append(pl.BlockSpec((None,) + arr.shape[1:],
                                  lambda *_, idx=idx, nd=nd: (idx,) + (0,) * (nd - 1)))
        args.append(arr)
    return specs, args


def _in_proj_kernel(x_ref, g_ref, w_ref, o_ref, *, chunk):
    xn = _rms(x_ref[...], g_ref[...]).astype(BF16)
    for c0 in range(0, IN_COLS_PAD, chunk):
        o_ref[:, c0:c0 + chunk] = jnp.dot(
            xn, w_ref[:, c0:c0 + chunk], preferred_element_type=F32).astype(BF16)


def in_proj(x, prm, l, tm=512, chunk=512):
    n = x.shape[0]
    specs, args = _layer_specs(prm, ["norm_mix", "w_cat"], l)
    return pl.pallas_call(
        functools.partial(_in_proj_kernel, chunk=chunk),
        out_shape=jax.ShapeDtypeStruct((n, IN_COLS_PAD), BF16),
        grid=(n // tm,),
        in_specs=[pl.BlockSpec((tm, D_MODEL), lambda i: (i, 0))] + specs,
        out_specs=pl.BlockSpec((tm, IN_COLS_PAD), lambda i: (i, 0)),
        compiler_params=_params("parallel"),
        name="in_proj",
    )(x, *args)


def _split_bf16(x):
    hi = x.astype(BF16)
    return hi, (x - hi.astype(F32)).astype(BF16)


def _headwise_mm(a, b, bd, exact):
    m = a.shape[0]
    outs = []
    for p in range(a.shape[1] // LANES):
        ln = slice(p * LANES, (p + 1) * LANES)
        ap, bp = a[:, ln], b[:, ln]
        if exact:
            ah, al = _split_bf16(ap)
            bh, bl = _split_bf16(bp)
            wh = jnp.concatenate([bh, bh], axis=0) * bd
            wl = jnp.concatenate([bl, bl], axis=0) * bd
            r = jnp.dot(jnp.concatenate([ah, al], axis=0), wh, preferred_element_type=F32)
            outs.append(r[:m] + r[m:] + jnp.dot(ah, wl, preferred_element_type=F32))
        else:
            bh = bp.astype(BF16)
            outs.append(jnp.dot(ap.astype(BF16), jnp.concatenate([bh, bh], axis=0) * bd,
                                preferred_element_type=F32))
    return jnp.concatenate(outs, axis=1)


def _tri_inverse(n_cat, bd, exact_merge):
    size, width = n_cat.shape
    row = lax.broadcasted_iota(jnp.int32, (size, width), 0)
    col = lax.broadcasted_iota(jnp.int32, (size, width), 1) % size
    eye = jnp.where(row == col, 1.0, 0.0)

    def same_block(b):
        return (row // b) == (col // b)

    a0 = jnp.where(same_block(8), n_cat, 0.0)
    d = eye + a0
    p = _headwise_mm(a0, a0, bd, True)
    d = d + _headwise_mm(d, p, bd, True)
    p = _headwise_mm(p, p, bd, True)
    d = d + _headwise_mm(d, p, bd, True)
    b = 8
    while b < size:
        e = jnp.where(same_block(2 * b) & jnp.logical_not(same_block(b)), n_cat, 0.0)
        d = d + _headwise_mm(d, _headwise_mm(e, d, bd, exact_merge), bd, exact_merge)
        b *= 2
    return d


def _rwkv_kernel(*refs, has_vres, tb, group):
    if has_vres:
        (rkv_ref, lglv_ref, lwla_ref, vfirst_ref, mu_rkv, mu_lglv, mu_lwla, w0, w2p, a0, a2p,
         g2p, kkw, kaw, rkw, gnw, gnb, hsum, v0, v2p,
         y_ref,
         c_rkv, c_lglv, c_lwla, s_ref, phi_s, psi_s, etot_s,
         r_s, k_s, v_s, kk_s, kka_s, lw_s, g_s, y_s, rhat_s, y0_s) = refs
    else:
        (rkv_ref, lglv_ref, lwla_ref, mu_rkv, mu_lglv, mu_lwla, w0, w2p, a0, a2p,
         g2p, kkw, kaw, rkw, gnw, gnb, hsum,
         y_ref, vfirst_out,
         c_rkv, c_lglv, c_lwla, s_ref, phi_s, psi_s, etot_s,
         r_s, k_s, v_s, kk_s, kka_s, lw_s, g_s, y_s, rhat_s, y0_s) = refs

    @pl.when(pl.program_id(1) == 0)
    def _():
        c_rkv[...] = jnp.zeros_like(c_rkv)
        c_lglv[...] = jnp.zeros_like(c_lglv)
        c_lwla[...] = jnp.zeros_like(c_lwla)
        s_ref[...] = jnp.zeros_like(s_ref)

    def mix(p_ref, carry, mu):
        p = p_ref[...].astype(F32)
        prev = _shift_rows(carry[...], p, 1)
        carry[...] = p[tb - 8:]
        return p + (prev - p) * mu[...]

    rkv = mix(rkv_ref, c_rkv, mu_rkv)
    lglv = mix(lglv_ref, c_lglv, mu_lglv)
    lwla = mix(lwla_ref, c_lwla, mu_lwla)
    r = rkv[:, :WIDTH]
    k = rkv[:, WIDTH:2 * WIDTH]
    v = rkv[:, 2 * WIDTH:]

    w = -_softplus(-(w0[...] + _dot(jnp.tanh(lwla), w2p[...]))) - 0.5
    lw_s[...] = -jnp.exp(w)
    a = _sigmoid(a0[...] + _dot(lwla, a2p[...]))
    g_s[...] = _dot(_sigmoid(lglv), g2p[...])
    if has_vres:
        v = v + (vfirst_ref[...].astype(F32) - v) * _sigmoid(v0[...] + _dot(lglv, v2p[...]))
    else:
        vfirst_out[...] = v.astype(BF16)
    hs = hsum[...]
    kk = k * kkw[...]
    kk = kk / jnp.maximum(jnp.sqrt(_dot_split(kk * kk, hs)), 1e-12)
    k = k * (1.0 + (a - 1.0) * kaw[...])
    r_s[...] = r
    k_s[...] = k
    v_s[...] = v
    kk_s[...] = kk
    kka_s[...] = kk * a

    cl = RW_CHUNK
    n_chunks = tb // cl
    row = lax.broadcasted_iota(jnp.int32, (cl, cl), 0)
    col = lax.broadcasted_iota(jnp.int32, (cl, cl), 1)
    tri_incl = jnp.where(row >= col, 1.0, 0.0).astype(BF16)
    row2 = lax.broadcasted_iota(jnp.int32, (2 * cl, 2 * LANES), 0)
    col2 = lax.broadcasted_iota(jnp.int32, (2 * cl, 2 * LANES), 1)
    m_mask = (row2 % cl - col2 % cl) >= jnp.where(row2 < cl, 1, 0)
    rr = lax.broadcasted_iota(jnp.int32, (LANES, LANES), 0)
    cc = lax.broadcasted_iota(jnp.int32, (LANES, LANES), 1)
    same_head = (rr // HEAD_DIM) == (cc // HEAD_DIM)
    lane_lo = cc < HEAD_DIM
    bd = jnp.where(same_head, 1.0, 0.0).astype(BF16)
    same_head_w = jnp.concatenate([same_head, same_head], axis=1)
    same_head_t = jnp.concatenate([same_head, same_head], axis=0)
    zeros_half = jnp.zeros((cl, LANES), F32)

    def group_pre(gi, carry):
        rts, ats, vvs, zs, ms, e_mids, e_ends = [], [], [], [], [], [], []
        for g in range(group):
            ci = gi * group + g
            rows = pl.ds(pl.multiple_of(ci * cl, cl), cl)
            lw = lw_s[rows, :]
            cum = _dot01_left(tri_incl, lw)
            mid = cum[cl // 2 - 1:cl // 2, :]
            cen = cum - mid
            e_pos = jnp.exp(cen)
            e_neg = jnp.exp(-cen)
            rt = r_s[rows, :] * e_pos
            at = -kk_s[rows, :] * jnp.exp(cen - lw)
            bt = kka_s[rows, :] * e_neg
            kt = k_s[rows, :] * e_neg
            rts.append(rt)
            ats.append(at)
            vvs.append(v_s[rows, :])
            e_mids.append(jnp.exp(mid))
            e_ends.append(jnp.exp(cen[cl - 1:cl, :]))
            etot_s[ci] = jnp.broadcast_to(jnp.exp(cum[cl - 1:cl, :]), (8, WIDTH))
            for p in range(N_PAIRS):
                ln = slice(p * LANES, (p + 1) * LANES)
                z = jnp.concatenate([bt[:, ln], kt[:, ln]], axis=0)
                zt = z.T
                zr = pltpu.roll(zt, HEAD_DIM, 1)
                wgt = jnp.concatenate([jnp.where(lane_lo, zt, zr), jnp.where(lane_lo, zr, zt)], axis=1)
                wgt = jnp.where(same_head_w, wgt, 0.0)
                m = _dot(jnp.concatenate([at[:, ln], rt[:, ln]], axis=0), wgt)
                zs.append(z)
                ms.append(jnp.where(m_mask, m, 0.0))
        rt = jnp.concatenate(rts, axis=1)
        at = jnp.concatenate(ats, axis=1)
        vv = jnp.concatenate(vvs, axis=1)
        e_mid = jnp.concatenate(e_mids, axis=1)
        a_ab = jnp.concatenate([m[:cl, :LANES] for m in ms], axis=1)
        a_kr = jnp.concatenate(
            [jnp.concatenate([m[:cl, LANES:] for m in ms], axis=1),
             jnp.concatenate([m[cl:, LANES:] for m in ms], axis=1)], axis=0)
        a_rb = jnp.concatenate([m[cl:, :LANES] for m in ms], axis=1)
        t_inv = _tri_inverse(a_ab, bd, True)
        akv = _headwise_mm(a_kr, vv, bd, False)
        at_t = _headwise_mm(t_inv, at, bd, True)
        u0 = _headwise_mm(t_inv, akv[:cl], bd, True)
        rhat = (rt + _headwise_mm(a_rb, at_t, bd, False)) * e_mid
        y0 = _headwise_mm(a_rb, u0, bd, False) + akv[cl:]
        at_m = at_t * e_mid
        for g in range(group):
            ci = gi * group + g
            rows = pl.ds(pl.multiple_of(ci * cl, cl), cl)
            rhat_s[rows, :] = rhat[:, g * WIDTH:(g + 1) * WIDTH]
            y0_s[rows, :] = y0[:, g * WIDTH:(g + 1) * WIDTH]
            for p in range(N_PAIRS):
                ln = slice(g * WIDTH + p * LANES, g * WIDTH + (p + 1) * LANES)
                lhs = jnp.concatenate(
                    [jnp.concatenate([at_m[:, ln], zeros_half], axis=0),
                     jnp.concatenate([u0[:, ln], vv[:, ln]], axis=0)], axis=1)
                pp = _dot(lhs.T, zs[g * N_PAIRS + p]) * e_ends[g][:, p * LANES:(p + 1) * LANES]
                pp = jnp.where(same_head_t, pp, 0.0)
                phi_s[ci, p] = pp[:LANES]
                psi_s[ci, p] = pp[LANES:]
        return carry

    if n_chunks == group:
        group_pre(0, 0)
    else:
        lax.fori_loop(0, n_chunks // group, group_pre, 0)

    states = [s_ref[p] for p in range(N_PAIRS)]
    for ci in range(n_chunks):
        rows = slice(ci * cl, (ci + 1) * cl)
        et = etot_s[ci][0:1]
        for p in range(N_PAIRS):
            ln = slice(p * LANES, (p + 1) * LANES)
            s0 = states[p]
            y_s[rows, ln] = _dot_nt(rhat_s[rows, ln], s0) + y0_s[rows, ln]
            states[p] = s0 * et[:, ln] + _dot(s0, phi_s[ci, p]) + psi_s[ci, p]
    for p in range(N_PAIRS):
        s_ref[p] = states[p]

    y = y_s[...]
    inv_n = 1.0 / HEAD_DIM
    mean = _dot_split(y, hs) * inv_n
    yc = y - mean
    var = _dot_split(yc * yc, hs) * inv_n
    yn = yc * lax.rsqrt(var + RW_GN_EPS) * gnw[...] + gnb[...]
    bonus = _dot_split(r_s[...] * k_s[...] * rkw[...], hs) * v_s[...]
    y_ref[...] = ((yn + bonus) * g_s[...]).astype(BF16)


def rwkv_mixer(proj, vfirst, prm, l, batch, seq, tb=256, group=4):
    n = proj.shape[0]
    nt = seq // tb
    has_vres = vfirst is not None
    row = lambda b, t: b * nt + t
    in_specs = [pl.BlockSpec((tb, 3 * WIDTH), lambda b, t: (row(b, t), C_RKV // (3 * WIDTH))),
                pl.BlockSpec((tb, 256), lambda b, t: (row(b, t), C_LGLV // 256)),
                pl.BlockSpec((tb, 128), lambda b, t: (row(b, t), C_LWLA // 128))]
    args = [proj, proj, proj]
    if has_vres:
        in_specs.append(pl.BlockSpec((tb, WIDTH), lambda b, t: (row(b, t), 0)))
        args.append(vfirst)
    names = ["mu_rkv", "mu_lglv", "mu_lwla", "w0", "w2p", "a0", "a2p", "g2p", "kkw", "kaw", "rkw",
             "gnw", "gnb", "hsum"]
    if has_vres:
        names += ["v0", "v2p"]
    specs, pargs = _layer_specs(prm, names, l)
    in_specs += specs
    args += pargs
    y_spec = pl.BlockSpec((tb, WIDTH), lambda b, t: (row(b, t), 0))
    y_shape = jax.ShapeDtypeStruct((n, WIDTH), BF16)
    if has_vres:
        out_shape, out_specs = y_shape, y_spec
    else:
        out_shape, out_specs = (y_shape, y_shape), (y_spec, y_spec)
    big = pltpu.VMEM((tb, WIDTH), F32)
    nc = tb // RW_CHUNK
    pair_mats = pltpu.VMEM((nc, N_PAIRS, LANES, LANES), F32)
    scratch = [pltpu.VMEM((8, 3 * WIDTH), F32), pltpu.VMEM((8, 256), F32), pltpu.VMEM((8, 128), F32),
               pltpu.VMEM((N_PAIRS, LANES, LANES), F32), pair_mats, pair_mats,
               pltpu.VMEM((nc, 8, WIDTH), F32)] + [big] * 10
    return pl.pallas_call(
        functools.partial(_rwkv_kernel, has_vres=has_vres, tb=tb, group=group),
        out_shape=out_shape, grid=(batch, nt), in_specs=in_specs, out_specs=out_specs,
        scratch_shapes=scratch, compiler_params=_params("parallel", "arbitrary"),
        name="rwkv7_vres" if has_vres else "rwkv7",
    )(*args)


def _ssd_kernel(zx_ref, dt_ref, convw, convb, dtb, a_neg, dskip, normw, exp_ch, exp_ln,
                y_ref, c_conv, st_ref, *, nb):
    q = SSM_CHUNK
    rows_all = nb * q

    @pl.when(pl.program_id(1) == 0)
    def _():
        c_conv[...] = jnp.zeros_like(c_conv)
        st_ref[...] = jnp.zeros_like(st_ref)

    zx = zx_ref[...].astype(F32)
    z_all = zx[:, :WIDTH]
    xbc = zx[:, WIDTH:]
    carry = c_conv[...]
    conv = xbc * convw[SSM_CONV - 1:SSM_CONV, :] + convb[...]
    for j in range(1, SSM_CONV):
        conv = conv + _shift_rows(carry, xbc, j) * convw[SSM_CONV - 1 - j:SSM_CONV - j, :]
    c_conv[...] = xbc[rows_all - 8:]
    xbc = conv * _sigmoid(conv)
    dt_all = _softplus(dt_ref[...].astype(F32) + dtb[...])
    a_all = dt_all * a_neg[...]
    row = lax.broadcasted_iota(jnp.int32, (q, q), 0)
    col = lax.broadcasted_iota(jnp.int32, (q, q), 1)
    causal = row >= col
    tri = jnp.where(causal, 1.0, 0.0).astype(BF16)
    ech = exp_ch[...]
    eln = exp_ln[...]
    lo = lax.broadcasted_iota(jnp.int32, (q, LANES), 1) < HEAD_DIM
    per_g = N_HEADS // SSM_GROUPS
    gw = per_g * HEAD_DIM
    states = [st_ref[g] for g in range(SSM_GROUPS)]
    for sb in range(nb):
        rs = slice(sb * q, (sb + 1) * q)
        xs = xbc[rs, :WIDTH]
        bm = xbc[rs, WIDTH:WIDTH + SSM_GROUPS * SSM_STATE]
        cm = xbc[rs, WIDTH + SSM_GROUPS * SSM_STATE:]
        dt = dt_all[rs]
        acum = _dot01_left(tri, a_all[rs])
        acum_t = acum.T
        dt_x = _dot_split(dt, ech)
        acum_x = _dot01_right(acum, ech)
        atot_x = acum_x[q - 1:q, :]
        acum_l = _dot01_right(acum, eln)
        xdt = xs * dt_x
        xdec = xdt * jnp.exp(atot_x - acum_x)
        e_in = jnp.exp(acum_x)
        e_tot = jnp.exp(atot_x)
        ys = []
        for g in range(SSM_GROUPS):
            bg = bm[:, g * SSM_STATE:(g + 1) * SSM_STATE]
            cg = cm[:, g * SSM_STATE:(g + 1) * SSM_STATE]
            cb = _dot_nt(cg, bg)
            gl = slice(g * gw, (g + 1) * gw)
            y_g = _dot(cg, states[g]) * e_in[:, gl]
            states[g] = states[g] * e_tot[:, gl] + _dot(bg.T, xdec[:, gl])
            diag = []
            for pp in range(per_g // 2):
                xp = xdt[:, gl][:, pp * LANES:(pp + 1) * LANES]
                acc = None
                for hh in range(2):
                    h = g * per_g + 2 * pp + hh
                    diff = acum_l[:, h * LANES:(h + 1) * LANES] - acum_t[h:h + 1, :]
                    lmat = jnp.where(causal, jnp.exp(jnp.where(causal, diff, 0.0)), 0.0)
                    xh = jnp.where(lo, xp, 0.0) if hh == 0 else jnp.where(lo, 0.0, xp)
                    t = _dot(cb * lmat, xh)
                    acc = t if acc is None else acc + t
                diag.append(acc)
            ys.append(y_g + jnp.concatenate(diag, axis=1))
        y = jnp.concatenate(ys, axis=1) + xs * dskip[...]
        z = z_all[rs]
        yz = y * (z * _sigmoid(z))
        outs = []
        for g in range(SSM_GROUPS):
            yg = yz[:, g * gw:(g + 1) * gw]
            outs.append(yg * lax.rsqrt(jnp.mean(yg * yg, axis=-1, keepdims=True) + NORM_EPS))
        y_ref[rs, :] = (jnp.concatenate(outs, axis=1) * normw[...]).astype(BF16)
    for g in range(SSM_GROUPS):
        st_ref[g] = states[g]


def ssd_mixer(proj, prm, l, batch, seq, nb=2):
    n = proj.shape[0]
    rows = nb * SSM_CHUNK
    nt = seq // rows
    row = lambda b, t: b * nt + t
    names = ["convw", "convb", "dtb", "a_neg", "dskip", "normw", "exp_ch", "exp_ln"]
    specs, args = _layer_specs(prm, names, l)
    in_specs = [pl.BlockSpec((rows, 3 * WIDTH), lambda b, t: (row(b, t), C_SSM // (3 * WIDTH))),
                pl.BlockSpec((rows, 128), lambda b, t: (row(b, t), C_DT // 128))] + specs
    return pl.pallas_call(
        functools.partial(_ssd_kernel, nb=nb),
        out_shape=jax.ShapeDtypeStruct((n, WIDTH), BF16),
        grid=(batch, nt), in_specs=in_specs,
        out_specs=pl.BlockSpec((rows, WIDTH), lambda b, t: (row(b, t), 0)),
        scratch_shapes=[pltpu.VMEM((8, WIDTH + 2 * SSM_GROUPS * SSM_STATE), F32),
                        pltpu.VMEM((SSM_GROUPS, SSM_STATE, WIDTH // SSM_GROUPS), F32)],
        compiler_params=_params("parallel", "arbitrary"),
        name="ssd",
    )(proj, proj, *args)


def _swa_kernel(sink_ref, qkv_ref, y_ref, kv_prev, *, nb, layer):
    wdw = WINDOW

    @pl.when(pl.program_id(1) == 0)
    def _():
        kv_prev[...] = jnp.zeros_like(kv_prev)

    qkv = qkv_ref[...].astype(F32)
    qi = lax.broadcasted_iota(jnp.int32, (wdw, 2 * wdw), 0)
    kj = lax.broadcasted_iota(jnp.int32, (wdw, 2 * wdw), 1)
    rel = qi + wdw - kj
    in_window = (rel >= 0) & (rel < wdw)
    first = jnp.where(pl.program_id(1) > 0, 0, wdw)
    lo = lax.broadcasted_iota(jnp.int32, (wdw, LANES), 1) < HEAD_DIM
    lo2 = lax.broadcasted_iota(jnp.int32, (2 * wdw, LANES), 1) < HEAD_DIM
    rep = N_HEADS // ATT_KV_HEADS
    scale = HEAD_DIM ** -0.5
    bands, scores = [], []
    prev = kv_prev[...]
    for sb in range(nb):
        rs = slice(sb * wdw, (sb + 1) * wdw)
        kv = qkv[rs, WIDTH:]
        band = jnp.concatenate([prev, kv], axis=0)
        prev = kv
        bands.append(band)
        valid = in_window & (kj >= first) if sb == 0 else in_window
        for h in range(N_HEADS):
            g = h // rep
            qp = qkv[rs, (h // 2) * LANES:(h // 2 + 1) * LANES]
            qh = jnp.where(lo, qp, 0.0) if h % 2 == 0 else jnp.where(lo, 0.0, qp)
            s = _dot_nt(qh, band[:, g * LANES:(g + 1) * LANES]) * scale
            scores.append(jnp.where(valid, s, -1e30))
    kv_prev[...] = prev
    probs = []
    for i, s in enumerate(scores):
        sink = sink_ref[layer, i % N_HEADS]
        m = jnp.maximum(jnp.max(s, axis=-1, keepdims=True), sink)
        p = jnp.exp(s - m)
        den = jnp.sum(p, axis=-1, keepdims=True) + jnp.exp(sink - m)
        probs.append(p / den)
    for sb in range(nb):
        outs = []
        for tile in range(N_HEADS // 2):
            g = (2 * tile) // rep
            vb = bands[sb][:, (ATT_KV_HEADS + g) * LANES:(ATT_KV_HEADS + g + 1) * LANES]
            outs.append(_dot(probs[sb * N_HEADS + 2 * tile], jnp.where(lo2, vb, 0.0))
                        + _dot(probs[sb * N_HEADS + 2 * tile + 1], jnp.where(lo2, 0.0, vb)))
        y_ref[sb * wdw:(sb + 1) * wdw, :] = jnp.concatenate(outs, axis=1).astype(BF16)


def swa_mixer(proj, sinks, l, batch, seq, nb=2):
    n = proj.shape[0]
    rows = nb * WINDOW
    nt = seq // rows
    row = lambda b, t: b * nt + t
    return pl.pallas_call(
        functools.partial(_swa_kernel, nb=nb, layer=l),
        out_shape=jax.ShapeDtypeStruct((n, WIDTH), BF16),
        grid=(batch, nt),
        in_specs=[pl.BlockSpec(memory_space=pltpu.SMEM),
                  pl.BlockSpec((rows, 2 * WIDTH), lambda b, t: (row(b, t), C_ATT // (2 * WIDTH)))],
        out_specs=pl.BlockSpec((rows, WIDTH), lambda b, t: (row(b, t), 0)),
        scratch_shapes=[pltpu.VMEM((WINDOW, WIDTH), F32)],
        compiler_params=_params("parallel", "arbitrary"),
        name="swa",
    )(sinks, proj)


def _merge_kernel(x_ref, gate_ref, yrw, yssm, yatt, gb, wrw, wssm, watt, wout, o_ref):
    merged = None
    for i, (y, w) in enumerate(((yrw, wrw), (yssm, wssm), (yatt, watt))):
        cols = slice(i * D_MODEL, (i + 1) * D_MODEL)
        gate = _sigmoid(gate_ref[:, cols].astype(F32) + gb[:, cols])
        t = gate * jnp.dot(y[...], w[...], preferred_element_type=F32)
        merged = t if merged is None else merged + t
    o_ref[...] = x_ref[...] + _dot(merged, wout[...])


def merge_out(x, proj, y_rw, y_ssm, y_att, prm, l, tm=512):
    n = x.shape[0]
    names = ["gate_b", "w_br_rw", "w_br_ssm", "w_br_att", "w_out"]
    specs, args = _layer_specs(prm, names, l)
    rows = lambda width: pl.BlockSpec((tm, width), lambda i: (i, 0))
    return pl.pallas_call(
        _merge_kernel,
        out_shape=jax.ShapeDtypeStruct((n, D_MODEL), F32),
        grid=(n // tm,),
        in_specs=[rows(D_MODEL), rows(3 * D_MODEL), rows(WIDTH), rows(WIDTH), rows(WIDTH)] + specs,
        out_specs=rows(D_MODEL),
        compiler_params=_params("parallel"),
        name="merge_out",
    )(x, proj, y_rw, y_ssm, y_att, *args)


def _ffn_kernel(x_ref, gamma, wgu, wd, gfin, o_ref, act, *, chunk, final_norm):
    x = x_ref[...]
    xn = _rms(x, gamma[...]).astype(BF16)
    for c0 in range(0, FFN_HIDDEN, chunk):
        gate = jnp.dot(xn, wgu[:, c0:c0 + chunk], preferred_element_type=F32)
        up = jnp.dot(xn, wgu[:, FFN_HIDDEN + c0:FFN_HIDDEN + c0 + chunk], preferred_element_type=F32)
        act[:, c0:c0 + chunk] = (gate * _sigmoid(gate) * up).astype(BF16)
    out = x + jnp.dot(act[...], wd[...], preferred_element_type=F32)
    if final_norm:
        out = _rms(out, gfin[...])
    o_ref[...] = out


def ffn(x, prm, l, final_norm, tm=512, chunk=256):
    n = x.shape[0]
    specs, args = _layer_specs(prm, ["norm_ffn", "w_gu", "w_down"], l)
    fspecs, fargs = _layer_specs(prm, ["norm_final"], 0)
    return pl.pallas_call(
        functools.partial(_ffn_kernel, chunk=chunk, final_norm=final_norm),
        out_shape=jax.ShapeDtypeStruct((n, D_MODEL), F32),
        grid=(n // tm,),
        in_specs=[pl.BlockSpec((tm, D_MODEL), lambda i: (i, 0))] + specs + fspecs,
        out_specs=pl.BlockSpec((tm, D_MODEL), lambda i: (i, 0)),
        scratch_shapes=[pltpu.VMEM((tm, FFN_HIDDEN), BF16)],
        compiler_params=_params("parallel"),
        name="ffn_final" if final_norm else "ffn",
    )(x, *args, *fargs)


def _pad_axis(w, size, axis, at=0):
    pads = [(0, 0)] * w.ndim
    pads[axis] = (at, size - at - w.shape[axis])
    return jnp.pad(w, pads)


def _prep_params(p):
    depth = p["w_in"].shape[0]
    rw_cols = 3 * WIDTH + RW_DECAY_LORA + RW_ICLR_LORA + RW_GATE_LORA
    ssm_cols = WIDTH + (WIDTH + 2 * SSM_GROUPS * SSM_STATE) + N_HEADS
    att_cols = WIDTH + 2 * ATT_KV_HEADS * HEAD_DIM
    o_ssm = rw_cols
    o_att = o_ssm + ssm_cols
    o_gate = o_att + att_cols
    o_k = o_att + WIDTH
    o_v = o_k + ATT_KV_HEADS * HEAD_DIM
    w = p["w_in"]
    zeros = lambda c: jnp.zeros((depth, D_MODEL, c), w.dtype)
    dup = lambda o: [w[:, :, o + HEAD_DIM * (i // 2):o + HEAD_DIM * (i // 2 + 1)] for i in range(4)]
    lv = jnp.concatenate([zeros(RW_VRES_LORA)[:1], p["rw_vres_down"]], axis=0)
    pad_lglv = 256 - RW_GATE_LORA - RW_VRES_LORA
    w_cat = jnp.concatenate(
        [w[:, :, o_gate:o_gate + 3 * D_MODEL],
         w[:, :, :3 * WIDTH],
         w[:, :, o_ssm:o_ssm + 3 * WIDTH],
         w[:, :, o_att:o_att + WIDTH]] + dup(o_k) + dup(o_v)
        + [w[:, :, 3 * WIDTH + 128:rw_cols], lv, zeros(pad_lglv),
           w[:, :, 3 * WIDTH:3 * WIDTH + 128],
           w[:, :, o_ssm + 3 * WIDTH:o_ssm + 3 * WIDTH + N_HEADS], zeros(128 - N_HEADS)], axis=2)
    assert w_cat.shape == (depth, D_MODEL, IN_COLS_PAD)

    row = lambda v: v.reshape(v.shape[0], 1, -1).astype(F32)
    mu = p["rw_mu"]
    vmu = jnp.concatenate([jnp.zeros((1, RW_VRES_LORA), F32), p["rw_vres_mu"]], axis=0)
    head_of = jnp.arange(WIDTH) // HEAD_DIM
    return {
        "norm_mix": row(p["norm_mix"]),
        "w_cat": w_cat.astype(BF16),
        "mu_rkv": row(mu[:, :3 * WIDTH]),
        "mu_lwla": row(mu[:, 3 * WIDTH:3 * WIDTH + 128]),
        "mu_lglv": row(jnp.concatenate([mu[:, 3 * WIDTH + 128:], vmu,
                                        jnp.zeros((depth, pad_lglv), F32)], axis=1)),
        "w0": row(p["rw_w0"]),
        "w2p": _pad_axis(p["rw_w2"], 128, 1, 0).astype(BF16),
        "a0": row(p["rw_a0"]),
        "a2p": _pad_axis(p["rw_a2"], 128, 1, RW_DECAY_LORA).astype(BF16),
        "g2p": _pad_axis(p["rw_g2"], 256, 1, 0).astype(BF16),
        "kkw": row(p["rw_k_k"]), "kaw": row(p["rw_k_a"]), "rkw": row(p["rw_r_k"]),
        "gnw": row(p["rw_gn_w"]), "gnb": row(p["rw_gn_b"]),
        "v0": row(p["rw_vres_v0"]),
        "v2p": _pad_axis(p["rw_vres_v2"], 256, 1, RW_GATE_LORA).astype(BF16),
        "hsum": (head_of[:, None] == head_of[None, :]).astype(BF16)[None],
        "convw": p["ssm_conv_w"].astype(F32),
        "convb": row(p["ssm_conv_b"]),
        "dtb": row(_pad_axis(p["ssm_dt_bias"], 128, 1)),
        "a_neg": row(_pad_axis(-jnp.exp(p["ssm_a_log"].astype(F32)), 128, 1)),
        "dskip": row(jnp.repeat(p["ssm_d"], HEAD_DIM, axis=1)),
        "normw": row(p["ssm_norm_w"]),
        "exp_ch": (jnp.arange(128)[:, None] == head_of[None, :]).astype(BF16)[None],
        "exp_ln": (jnp.arange(128)[:, None]
                   == (jnp.arange(N_HEADS * LANES) // LANES)[None, :]).astype(BF16)[None],
        "sinks": p["att_sinks"].astype(F32),
        "gate_b": row(p["gate_b"]),
        "w_br_rw": p["w_br_rw"].astype(BF16),
        "w_br_ssm": p["w_br_ssm"].astype(BF16),
        "w_br_att": p["w_br_att"].astype(BF16),
        "w_out": p["w_out"].astype(BF16),
        "norm_ffn": row(p["norm_ffn"]),
        "w_gu": p["ffn_w_gu"].astype(BF16),
        "w_down": p["ffn_w_down"].astype(BF16),
        "norm_final": p["norm_final"].reshape(1, 1, -1).astype(F32),
    }


def kernel(x, norm_mix, w_in, rw_mu, rw_w0, rw_w2, rw_a0, rw_a2, rw_g2, rw_k_k, rw_k_a, rw_r_k, rw_gn_w, rw_gn_b, rw_vres_down, rw_vres_mu, rw_vres_v0, rw_vres_v2, ssm_conv_w, ssm_conv_b, ssm_dt_bias, ssm_a_log, ssm_d, ssm_norm_w, att_sinks, gate_b, w_br_rw, w_br_ssm, w_br_att, w_out, norm_ffn, ffn_w_gu, ffn_w_down, norm_final):
    p = dict(norm_mix=norm_mix, w_in=w_in, rw_mu=rw_mu, rw_w0=rw_w0, rw_w2=rw_w2, rw_a0=rw_a0,
             rw_a2=rw_a2, rw_g2=rw_g2, rw_k_k=rw_k_k, rw_k_a=rw_k_a,
             rw_r_k=rw_r_k.reshape(rw_r_k.shape[0], -1), rw_gn_w=rw_gn_w, rw_gn_b=rw_gn_b,
             rw_vres_down=rw_vres_down, rw_vres_mu=rw_vres_mu, rw_vres_v0=rw_vres_v0,
             rw_vres_v2=rw_vres_v2, ssm_conv_w=ssm_conv_w, ssm_conv_b=ssm_conv_b,
             ssm_dt_bias=ssm_dt_bias, ssm_a_log=ssm_a_log, ssm_d=ssm_d, ssm_norm_w=ssm_norm_w,
             att_sinks=att_sinks, gate_b=gate_b, w_br_rw=w_br_rw, w_br_ssm=w_br_ssm,
             w_br_att=w_br_att, w_out=w_out, norm_ffn=norm_ffn, ffn_w_gu=ffn_w_gu,
             ffn_w_down=ffn_w_down, norm_final=norm_final)
    batch, seq, _ = x.shape
    depth = w_in.shape[0]
    prm = _prep_params(p)
    xf = x.reshape(batch * seq, D_MODEL)
    vfirst = None
    for l in range(depth):
        proj = in_proj(xf, prm, l)
        if l == 0:
            y_rw, vfirst = rwkv_mixer(proj, None, prm, l, batch, seq)
        else:
            y_rw = rwkv_mixer(proj, vfirst, prm, l, batch, seq)
        y_ssm = ssd_mixer(proj, prm, l, batch, seq)
        y_att = swa_mixer(proj, prm["sinks"], l, batch, seq)
        xf = merge_out(xf, proj, y_rw, y_ssm, y_att, prm, l)
        xf = ffn(xf, prm, l, final_norm=(l == depth - 1))
    return xf.reshape(batch, seq, D_MODEL)
```

```python
import functools

import jax
import jax.numpy as jnp
from jax import lax
from jax.experimental import pallas as pl
from jax.experimental.pallas import tpu as pltpu

F32 = jnp.float32
BF16 = jnp.bfloat16

D_MODEL = 1024
HEAD_DIM = 64
N_HEADS = 8
WIDTH = N_HEADS * HEAD_DIM
LANES = 128
N_PAIRS = WIDTH // LANES
RW_CHUNK = 64
RW_DECAY_LORA = 64
RW_ICLR_LORA = 64
RW_VRES_LORA = 32
RW_GATE_LORA = 160
RW_GN_EPS = 64e-5
SSM_STATE = 128
SSM_GROUPS = 2
SSM_CHUNK = 128
SSM_CONV = 4
ATT_KV_HEADS = 2
WINDOW = 128
FFN_HIDDEN = 2816
NORM_EPS = 1e-6
VMEM_LIMIT = 56 * 1024 * 1024

C_GATE = 0
C_RKV = 3072
C_SSM = 4608
C_ATT = 6144
C_LGLV = 7168
C_LWLA = 7424
C_DT = 7552
IN_COLS_PAD = 7680


def _dot(a, b):
    return jnp.dot(a.astype(BF16), b.astype(BF16), preferred_element_type=F32)


def _dot_nt(a, b):
    return lax.dot_general(a.astype(BF16), b.astype(BF16), (((1,), (1,)), ((), ())),
                           preferred_element_type=F32)


def _split3(x):
    hi = x.astype(BF16)
    r1 = x - hi.astype(F32)
    mid = r1.astype(BF16)
    return hi, mid, (r1 - mid.astype(F32)).astype(BF16)


def _dot01_left(m01, x):
    return sum(jnp.dot(m01, part, preferred_element_type=F32) for part in _split3(x))


def _dot01_right(x, m01):
    return sum(jnp.dot(part, m01, preferred_element_type=F32) for part in _split3(x))


def _dot_split(a, b01):
    hi = a.astype(BF16)
    lo = (a - hi.astype(F32)).astype(BF16)
    return (jnp.dot(hi, b01, preferred_element_type=F32)
            + jnp.dot(lo, b01, preferred_element_type=F32))


def _sigmoid(x):
    return 1.0 / (1.0 + jnp.exp(-x))


def _softplus(x):
    return jnp.maximum(x, 0.0) + jnp.log(1.0 + jnp.exp(-jnp.abs(x)))


def _rms(x, w):
    return x * lax.rsqrt(jnp.mean(x * x, axis=-1, keepdims=True) + NORM_EPS) * w


def _shift_rows(carry8, p, j):
    ext = jnp.concatenate([carry8, p], axis=0)
    return pltpu.roll(ext, j, 0)[8:]


def _params(*sem):
    return pltpu.CompilerParams(dimension_semantics=sem, vmem_limit_bytes=VMEM_LIMIT)


_SHARED = ("hsum", "exp_ch", "exp_ln")
_FROM_SECOND_LAYER = ("v0", "v2p")


def _layer_specs(prm, names, l):
    specs, args = [], []
    for nm in names:
        arr = prm[nm]
        idx = 0 if nm in _SHARED else (l - 1 if nm in _FROM_SECOND_LAYER else l)
        nd = arr.ndim
        specs.append(pl.BlockSpec((None,) + arr.shape[1:],
                                  lambda *_, idx=idx, nd=nd: (idx,) + (0,) * (nd - 1),
                                  pipeline_mode=pl.Buffered(1)))
        args.append(arr)
    return specs, args


def _in_proj_kernel(x_ref, g_ref, w_ref, o_ref, *, chunk):
    xn = _rms(x_ref[...], g_ref[...]).astype(BF16)
    for c0 in range(0, IN_COLS_PAD, chunk):
        o_ref[:, c0:c0 + chunk] = jnp.dot(
            xn, w_ref[:, c0:c0 + chunk], preferred_element_type=F32).astype(BF16)


def in_proj(x, prm, l, tm=512, chunk=512):
    n = x.shape[0]
    specs, args = _layer_specs(prm, ["norm_mix", "w_cat"], l)
    return pl.pallas_call(
        functools.partial(_in_proj_kernel, chunk=chunk),
        out_shape=jax.ShapeDtypeStruct((n, IN_COLS_PAD), BF16),
        grid=(n // tm,),
        in_specs=[pl.BlockSpec((tm, D_MODEL), lambda i: (i, 0))] + specs,
        out_specs=pl.BlockSpec((tm, IN_COLS_PAD), lambda i: (i, 0)),
        compiler_params=_params("parallel"),
        name="in_proj",
    )(x, *args)


def _split_bf16(x):
    hi = x.astype(BF16)
    return hi, (x - hi.astype(F32)).astype(BF16)


def _pair_tiles(xb, bd):
    return [jnp.concatenate([xb[:, i:i + LANES]] * 2, axis=0) * bd
            for i in range(0, xb.shape[1], LANES)]


def _mm_tiles(lhs, tiles):
    stacked = lhs[0] if len(lhs) == 1 else jnp.concatenate(lhs, axis=0)
    return jnp.concatenate(
        [jnp.dot(stacked[:, p * LANES:(p + 1) * LANES], t, preferred_element_type=F32)
         for p, t in enumerate(tiles)], axis=1)


def _mm_split(a_hl, b_tiles_hl, m):
    r1 = _mm_tiles(list(a_hl), b_tiles_hl[0])
    return r1[:m] + r1[m:] + _mm_tiles([a_hl[0]], b_tiles_hl[1])


def _headwise_mm(a, b, bd):
    return _mm_tiles([a.astype(BF16)], _pair_tiles(b.astype(BF16), bd))


def _tri_inverse(n_cat, bd):
    size, width = n_cat.shape
    row = lax.broadcasted_iota(jnp.int32, (size, width), 0)
    col = lax.broadcasted_iota(jnp.int32, (size, width), 1) % size

    def same_block(b):
        return (row // b) == (col // b)

    a0 = jnp.where(same_block(8), n_cat, 0.0).astype(BF16)
    d0 = jnp.where(row == col, 1.0, 0.0).astype(BF16) + a0
    p = _mm_tiles([a0], _pair_tiles(a0, bd))
    p_hl = _split_bf16(p)
    p_tiles = (_pair_tiles(p_hl[0], bd), _pair_tiles(p_hl[1], bd))
    r1 = _mm_tiles([d0, p_hl[0], p_hl[1]], p_tiles[0])
    r2 = _mm_tiles([d0, p_hl[0]], p_tiles[1])
    d = d0.astype(F32) + r1[:size] + r2[:size]
    p = r1[size:2 * size] + r1[2 * size:] + r2[size:]
    p_hl = _split_bf16(p)
    d = d + _mm_split(_split_bf16(d), (_pair_tiles(p_hl[0], bd), _pair_tiles(p_hl[1], bd)), size)
    b = 8
    while b < size:
        e = jnp.where(same_block(2 * b) & jnp.logical_not(same_block(b)), n_cat, 0.0).astype(BF16)
        d_hl = _split_bf16(d)
        r = _mm_tiles(list(d_hl), _pair_tiles(e, bd))
        x_hl = _split_bf16(r[:size] + r[size:])
        d = d + _mm_split(x_hl, (_pair_tiles(d_hl[0], bd), _pair_tiles(d_hl[1], bd)), size)
        b *= 2
    return d


def _rwkv_kernel(*refs, has_vres, tb, group):
    if has_vres:
        (rkv_ref, lglv_ref, lwla_ref, vfirst_ref, mu_rkv, mu_lglv, mu_lwla, w0, w2p, a0, a2p,
         g2p, kkw, kaw, rkw, gnw, gnb, hsum, v0, v2p,
         y_ref,
         c_rkv, c_lglv, c_lwla, s_ref, phi_s, psi_s, etot_s,
         r_s, k_s, v_s, kk_s, kka_s, lw_s, g_s, y_s, rhat_s, y0_s) = refs
    else:
        (rkv_ref, lglv_ref, lwla_ref, mu_rkv, mu_lglv, mu_lwla, w0, w2p, a0, a2p,
         g2p, kkw, kaw, rkw, gnw, gnb, hsum,
         y_ref, vfirst_out,
         c_rkv, c_lglv, c_lwla, s_ref, phi_s, psi_s, etot_s,
         r_s, k_s, v_s, kk_s, kka_s, lw_s, g_s, y_s, rhat_s, y0_s) = refs

    @pl.when(pl.program_id(1) == 0)
    def _():
        c_rkv[...] = jnp.zeros_like(c_rkv)
        c_lglv[...] = jnp.zeros_like(c_lglv)
        c_lwla[...] = jnp.zeros_like(c_lwla)
        s_ref[...] = jnp.zeros_like(s_ref)

    def mix(p_ref, carry, mu):
        p = p_ref[...].astype(F32)
        prev = _shift_rows(carry[...], p, 1)
        carry[...] = p[tb - 8:]
        return p + (prev - p) * mu[...]

    rkv = mix(rkv_ref, c_rkv, mu_rkv)
    lglv = mix(lglv_ref, c_lglv, mu_lglv)
    lwla = mix(lwla_ref, c_lwla, mu_lwla)
    r = rkv[:, :WIDTH]
    k = rkv[:, WIDTH:2 * WIDTH]
    v = rkv[:, 2 * WIDTH:]

    w = -_softplus(-(w0[...] + _dot(jnp.tanh(lwla), w2p[...]))) - 0.5
    lw_s[...] = -jnp.exp(w)
    a = _sigmoid(a0[...] + _dot(lwla, a2p[...]))
    g_s[...] = _dot(_sigmoid(lglv), g2p[...])
    if has_vres:
        v = v + (vfirst_ref[...].astype(F32) - v) * _sigmoid(v0[...] + _dot(lglv, v2p[...]))
    else:
        vfirst_out[...] = v.astype(BF16)
    hs = hsum[...]
    kk = k * kkw[...]
    kk = kk / jnp.maximum(jnp.sqrt(_dot(kk * kk, hs)), 1e-12)
    k = k * (1.0 + (a - 1.0) * kaw[...])
    r_s[...] = r
    k_s[...] = k
    v_s[...] = v
    kk_s[...] = kk
    kka_s[...] = kk * a

    cl = RW_CHUNK
    n_chunks = tb // cl
    row = lax.broadcasted_iota(jnp.int32, (cl, cl), 0)
    col = lax.broadcasted_iota(jnp.int32, (cl, cl), 1)
    tri_incl = jnp.where(row >= col, 1.0, 0.0).astype(BF16)
    row2 = lax.broadcasted_iota(jnp.int32, (2 * cl, 2 * LANES), 0)
    col2 = lax.broadcasted_iota(jnp.int32, (2 * cl, 2 * LANES), 1)
    m_mask = (row2 % cl - col2 % cl) >= jnp.where(row2 < cl, 1, 0)
    rr = lax.broadcasted_iota(jnp.int32, (LANES, LANES), 0)
    cc = lax.broadcasted_iota(jnp.int32, (LANES, LANES), 1)
    same_head = (rr // HEAD_DIM) == (cc // HEAD_DIM)
    lane_lo = cc < HEAD_DIM
    bd = jnp.where(same_head, 1.0, 0.0).astype(BF16)
    same_head_w = jnp.concatenate([same_head, same_head], axis=1)
    same_head_t = jnp.concatenate([same_head, same_head], axis=0)
    zeros_half = jnp.zeros((cl, LANES), F32)

    def group_pre(gi, carry):
        chunks = [gi * group + g for g in range(group)]
        rows = [pl.ds(pl.multiple_of(ci * cl, cl), cl) for ci in chunks]
        lws = [lw_s[r, :] for r in rows]
        cums = [_dot01_left(tri_incl, lw) for lw in lws]
        rts, ats, bts, kts, e_mids, e_ends = [], [], [], [], [], []
        for g in range(group):
            cum, lw = cums[g], lws[g]
            mid = cum[cl // 2 - 1:cl // 2, :]
            cen = cum - mid
            e_pos = jnp.exp(cen)
            e_neg = jnp.exp(-cen)
            rts.append(r_s[rows[g], :] * e_pos)
            ats.append(-kk_s[rows[g], :] * jnp.exp(cen - lw))
            bts.append(kka_s[rows[g], :] * e_neg)
            kts.append(k_s[rows[g], :] * e_neg)
            e_mids.append(jnp.exp(mid))
            e_ends.append(jnp.exp(cen[cl - 1:cl, :]))
            etot_s[chunks[g]] = jnp.broadcast_to(jnp.exp(cum[cl - 1:cl, :]), (8, WIDTH))
        pairs = [(g, p) for g in range(group) for p in range(N_PAIRS)]
        lane = lambda p: slice(p * LANES, (p + 1) * LANES)
        zs = [jnp.concatenate([bts[g][:, lane(p)], kts[g][:, lane(p)]], axis=0)
              for g, p in pairs]
        zts = [z.T for z in zs]
        ms = []
        for (g, p), zt in zip(pairs, zts):
            zr = pltpu.roll(zt, HEAD_DIM, 1)
            wgt = jnp.concatenate([jnp.where(lane_lo, zt, zr), jnp.where(lane_lo, zr, zt)], axis=1)
            wgt = jnp.where(same_head_w, wgt, 0.0)
            m = _dot(jnp.concatenate([ats[g][:, lane(p)], rts[g][:, lane(p)]], axis=0), wgt)
            ms.append(jnp.where(m_mask, m, 0.0))
        rt = jnp.concatenate(rts, axis=1)
        at = jnp.concatenate(ats, axis=1)
        vv = jnp.concatenate([v_s[r, :] for r in rows], axis=1)
        e_mid = jnp.concatenate(e_mids, axis=1)
        a_ab = jnp.concatenate([m[:cl, :LANES] for m in ms], axis=1)
        a_kr = jnp.concatenate(
            [jnp.concatenate([m[:cl, LANES:] for m in ms], axis=1),
             jnp.concatenate([m[cl:, LANES:] for m in ms], axis=1)], axis=0)
        a_rb = jnp.concatenate([m[cl:, :LANES] for m in ms], axis=1).astype(BF16)
        t_hl = _split_bf16(_tri_inverse(a_ab, bd))
        akv = _headwise_mm(a_kr, vv, bd)
        at_hl = _split_bf16(at)
        av_hl = _split_bf16(akv[:cl])
        at_t = _mm_split(t_hl, (_pair_tiles(at_hl[0], bd), _pair_tiles(at_hl[1], bd)), cl)
        u0 = _mm_split(t_hl, (_pair_tiles(av_hl[0], bd), _pair_tiles(av_hl[1], bd)), cl)
        rhat = (rt + _mm_tiles([a_rb], _pair_tiles(at_t.astype(BF16), bd))) * e_mid
        y0 = _mm_tiles([a_rb], _pair_tiles(u0.astype(BF16), bd)) + akv[cl:]
        at_m = at_t * e_mid
        for g in range(group):
            rhat_s[rows[g], :] = rhat[:, g * WIDTH:(g + 1) * WIDTH]
            y0_s[rows[g], :] = y0[:, g * WIDTH:(g + 1) * WIDTH]
        for i, (g, p) in enumerate(pairs):
            ln = slice(g * WIDTH + p * LANES, g * WIDTH + (p + 1) * LANES)
            lhs = jnp.concatenate(
                [jnp.concatenate([at_m[:, ln], zeros_half], axis=0),
                 jnp.concatenate([u0[:, ln], vv[:, ln]], axis=0)], axis=1)
            pp = _dot(lhs.T, zs[i]) * e_ends[g][:, lane(p)]
            pp = jnp.where(same_head_t, pp, 0.0)
            phi_s[chunks[g], p] = pp[:LANES]
            psi_s[chunks[g], p] = pp[LANES:]
        return carry

    if n_chunks == group:
        group_pre(0, 0)
    else:
        lax.fori_loop(0, n_chunks // group, group_pre, 0)

    states = [s_ref[p] for p in range(N_PAIRS)]
    for ci in range(n_chunks):
        rows = slice(ci * cl, (ci + 1) * cl)
        et = etot_s[ci][0:1]
        for p in range(N_PAIRS):
            ln = slice(p * LANES, (p + 1) * LANES)
            s0 = states[p]
            y_s[rows, ln] = _dot_nt(rhat_s[rows, ln], s0) + y0_s[rows, ln]
            states[p] = s0 * et[:, ln] + _dot(s0, phi_s[ci, p]) + psi_s[ci, p]
    for p in range(N_PAIRS):
        s_ref[p] = states[p]

    y = y_s[...]
    inv_n = 1.0 / HEAD_DIM
    mean = _dot(y, hs) * inv_n
    yc = y - mean
    var = _dot(yc * yc, hs) * inv_n
    yn = yc * lax.rsqrt(var + RW_GN_EPS) * gnw[...] + gnb[...]
    bonus = _dot(r_s[...] * k_s[...] * rkw[...], hs) * v_s[...]
    y_ref[...] = ((yn + bonus) * g_s[...]).astype(BF16)


def rwkv_mixer(proj, vfirst, prm, l, batch, seq, tb=512, group=8):
    n = proj.shape[0]
    nt = seq // tb
    has_vres = vfirst is not None
    row = lambda b, t: b * nt + t
    in_specs = [pl.BlockSpec((tb, 3 * WIDTH), lambda b, t: (row(b, t), C_RKV // (3 * WIDTH))),
                pl.BlockSpec((tb, 256), lambda b, t: (row(b, t), C_LGLV // 256)),
                pl.BlockSpec((tb, 128), lambda b, t: (row(b, t), C_LWLA // 128))]
    args = [proj, proj, proj]
    if has_vres:
        in_specs.append(pl.BlockSpec((tb, WIDTH), lambda b, t: (row(b, t), 0)))
        args.append(vfirst)
    names = ["mu_rkv", "mu_lglv", "mu_lwla", "w0", "w2p", "a0", "a2p", "g2p", "kkw", "kaw", "rkw",
             "gnw", "gnb", "hsum"]
    if has_vres:
        names += ["v0", "v2p"]
    specs, pargs = _layer_specs(prm, names, l)
    in_specs += specs
    args += pargs
    y_spec = pl.BlockSpec((tb, WIDTH), lambda b, t: (row(b, t), 0))
    y_shape = jax.ShapeDtypeStruct((n, WIDTH), BF16)
    if has_vres:
        out_shape, out_specs = y_shape, y_spec
    else:
        out_shape, out_specs = (y_shape, y_shape), (y_spec, y_spec)
    big = pltpu.VMEM((tb, WIDTH), F32)
    nc = tb // RW_CHUNK
    pair_mats = pltpu.VMEM((nc, N_PAIRS, LANES, LANES), F32)
    scratch = [pltpu.VMEM((8, 3 * WIDTH), F32), pltpu.VMEM((8, 256), F32), pltpu.VMEM((8, 128), F32),
               pltpu.VMEM((N_PAIRS, LANES, LANES), F32), pair_mats, pair_mats,
               pltpu.VMEM((nc, 8, WIDTH), F32)] + [big] * 10
    return pl.pallas_call(
        functools.partial(_rwkv_kernel, has_vres=has_vres, tb=tb, group=group),
        out_shape=out_shape, grid=(batch, nt), in_specs=in_specs, out_specs=out_specs,
        scratch_shapes=scratch, compiler_params=_params("parallel", "arbitrary"),
        name="rwkv7_vres" if has_vres else "rwkv7",
    )(*args)


def _ssd_kernel(zx_ref, dt_ref, convw, convb, dtb, a_neg, dskip, normw, exp_ch, exp_ln,
                y_ref, c_conv, st_ref, *, nb):
    q = SSM_CHUNK
    rows_all = nb * q

    @pl.when(pl.program_id(1) == 0)
    def _():
        c_conv[...] = jnp.zeros_like(c_conv)
        st_ref[...] = jnp.zeros_like(st_ref)

    zx = zx_ref[...].astype(F32)
    z_all = zx[:, :WIDTH]
    xbc = zx[:, WIDTH:]
    carry = c_conv[...]
    conv = xbc * convw[SSM_CONV - 1:SSM_CONV, :] + convb[...]
    for j in range(1, SSM_CONV):
        conv = conv + _shift_rows(carry, xbc, j) * convw[SSM_CONV - 1 - j:SSM_CONV - j, :]
    c_conv[...] = xbc[rows_all - 8:]
    xbc = conv * _sigmoid(conv)
    dt_all = _softplus(dt_ref[...].astype(F32) + dtb[...])
    a_all = dt_all * a_neg[...]
    row = lax.broadcasted_iota(jnp.int32, (q, q), 0)
    col = lax.broadcasted_iota(jnp.int32, (q, q), 1)
    causal = row >= col
    tri = jnp.where(causal, 1.0, 0.0).astype(BF16)
    ech = exp_ch[...]
    eln = exp_ln[...]
    lo = lax.broadcasted_iota(jnp.int32, (q, LANES), 1) < HEAD_DIM
    per_g = N_HEADS // SSM_GROUPS
    gw = per_g * HEAD_DIM
    states = [st_ref[g] for g in range(SSM_GROUPS)]
    for sb in range(nb):
        rs = slice(sb * q, (sb + 1) * q)
        xs = xbc[rs, :WIDTH]
        bm = xbc[rs, WIDTH:WIDTH + SSM_GROUPS * SSM_STATE]
        cm = xbc[rs, WIDTH + SSM_GROUPS * SSM_STATE:]
        dt = dt_all[rs]
        acum = _dot01_left(tri, a_all[rs])
        acum_t = acum.T
        dt_x = _dot_split(dt, ech)
        acum_x = _dot01_right(acum, ech)
        atot_x = acum_x[q - 1:q, :]
        acum_l = _dot01_right(acum, eln)
        xdt = xs * dt_x
        xdec = xdt * jnp.exp(atot_x - acum_x)
        e_in = jnp.exp(acum_x)
        e_tot = jnp.exp(atot_x)
        ys = []
        for g in range(SSM_GROUPS):
            bg = bm[:, g * SSM_STATE:(g + 1) * SSM_STATE]
            cg = cm[:, g * SSM_STATE:(g + 1) * SSM_STATE]
            cb = _dot_nt(cg, bg)
            gl = slice(g * gw, (g + 1) * gw)
            y_g = _dot(cg, states[g]) * e_in[:, gl]
            states[g] = states[g] * e_tot[:, gl] + _dot(bg.T, xdec[:, gl])
            diag = []
            for pp in range(per_g // 2):
                xp = xdt[:, gl][:, pp * LANES:(pp + 1) * LANES]
                acc = None
                for hh in range(2):
                    h = g * per_g + 2 * pp + hh
                    diff = acum_l[:, h * LANES:(h + 1) * LANES] - acum_t[h:h + 1, :]
                    lmat = jnp.where(causal, jnp.exp(jnp.where(causal, diff, 0.0)), 0.0)
                    xh = jnp.where(lo, xp, 0.0) if hh == 0 else jnp.where(lo, 0.0, xp)
                    t = _dot(cb * lmat, xh)
                    acc = t if acc is None else acc + t
                diag.append(acc)
            ys.append(y_g + jnp.concatenate(diag, axis=1))
        y = jnp.concatenate(ys, axis=1) + xs * dskip[...]
        z = z_all[rs]
        yz = y * (z * _sigmoid(z))
        outs = []
        for g in range(SSM_GROUPS):
            yg = yz[:, g * gw:(g + 1) * gw]
            outs.append(yg * lax.rsqrt(jnp.mean(yg * yg, axis=-1, keepdims=True) + NORM_EPS))
        y_ref[rs, :] = (jnp.concatenate(outs, axis=1) * normw[...]).astype(BF16)
    for g in range(SSM_GROUPS):
        st_ref[g] = states[g]


def ssd_mixer(proj, prm, l, batch, seq, nb=2):
    n = proj.shape[0]
    rows = nb * SSM_CHUNK
    nt = seq // rows
    row = lambda b, t: b * nt + t
    names = ["convw", "convb", "dtb", "a_neg", "dskip", "normw", "exp_ch", "exp_ln"]
    specs, args = _layer_specs(prm, names, l)
    in_specs = [pl.BlockSpec((rows, 3 * WIDTH), lambda b, t: (row(b, t), C_SSM // (3 * WIDTH))),
                pl.BlockSpec((rows, 128), lambda b, t: (row(b, t), C_DT // 128))] + specs
    return pl.pallas_call(
        functools.partial(_ssd_kernel, nb=nb),
        out_shape=jax.ShapeDtypeStruct((n, WIDTH), BF16),
        grid=(batch, nt), in_specs=in_specs,
        out_specs=pl.BlockSpec((rows, WIDTH), lambda b, t: (row(b, t), 0)),
        scratch_shapes=[pltpu.VMEM((8, WIDTH + 2 * SSM_GROUPS * SSM_STATE), F32),
                        pltpu.VMEM((SSM_GROUPS, SSM_STATE, WIDTH // SSM_GROUPS), F32)],
        compiler_params=_params("parallel", "arbitrary"),
        name="ssd",
    )(proj, proj, *args)


def _swa_kernel(sink_ref, qkv_ref, y_ref, kv_prev, *, nb, layer):
    wdw = WINDOW

    @pl.when(pl.program_id(1) == 0)
    def _():
        kv_prev[...] = jnp.zeros_like(kv_prev)

    qkv = qkv_ref[...].astype(F32)
    qi = lax.broadcasted_iota(jnp.int32, (wdw, 2 * wdw), 0)
    kj = lax.broadcasted_iota(jnp.int32, (wdw, 2 * wdw), 1)
    rel = qi + wdw - kj
    in_window = (rel >= 0) & (rel < wdw)
    first = jnp.where(pl.program_id(1) > 0, 0, wdw)
    lo = lax.broadcasted_iota(jnp.int32, (wdw, LANES), 1) < HEAD_DIM
    lo2 = lax.broadcasted_iota(jnp.int32, (2 * wdw, LANES), 1) < HEAD_DIM
    rep = N_HEADS // ATT_KV_HEADS
    scale = HEAD_DIM ** -0.5
    bands, scores = [], []
    prev = kv_prev[...]
    for sb in range(nb):
        rs = slice(sb * wdw, (sb + 1) * wdw)
        kv = qkv[rs, WIDTH:]
        band = jnp.concatenate([prev, kv], axis=0)
        prev = kv
        bands.append(band)
        valid = in_window & (kj >= first) if sb == 0 else in_window
        for h in range(N_HEADS):
            g = h // rep
            qp = qkv[rs, (h // 2) * LANES:(h // 2 + 1) * LANES]
            qh = jnp.where(lo, qp, 0.0) if h % 2 == 0 else jnp.where(lo, 0.0, qp)
            s = _dot_nt(qh, band[:, g * LANES:(g + 1) * LANES]) * scale
            scores.append(jnp.where(valid, s, -1e30))
    kv_prev[...] = prev
    probs = []
    for i, s in enumerate(scores):
        sink = sink_ref[layer, i % N_HEADS]
        m = jnp.maximum(jnp.max(s, axis=-1, keepdims=True), sink)
        p = jnp.exp(s - m)
        den = jnp.sum(p, axis=-1, keepdims=True) + jnp.exp(sink - m)
        probs.append(p / den)
    for sb in range(nb):
        outs = []
        for tile in range(N_HEADS // 2):
            g = (2 * tile) // rep
            vb = bands[sb][:, (ATT_KV_HEADS + g) * LANES:(ATT_KV_HEADS + g + 1) * LANES]
            outs.append(_dot(probs[sb * N_HEADS + 2 * tile], jnp.where(lo2, vb, 0.0))
                        + _dot(probs[sb * N_HEADS + 2 * tile + 1], jnp.where(lo2, 0.0, vb)))
        y_ref[sb * wdw:(sb + 1) * wdw, :] = jnp.concatenate(outs, axis=1).astype(BF16)


def swa_mixer(proj, sinks, l, batch, seq, nb=2):
    n = proj.shape[0]
    rows = nb * WINDOW
    nt = seq // rows
    row = lambda b, t: b * nt + t
    return pl.pallas_call(
        functools.partial(_swa_kernel, nb=nb, layer=l),
        out_shape=jax.ShapeDtypeStruct((n, WIDTH), BF16),
        grid=(batch, nt),
        in_specs=[pl.BlockSpec(memory_space=pltpu.SMEM),
                  pl.BlockSpec((rows, 2 * WIDTH), lambda b, t: (row(b, t), C_ATT // (2 * WIDTH)))],
        out_specs=pl.BlockSpec((rows, WIDTH), lambda b, t: (row(b, t), 0)),
        scratch_shapes=[pltpu.VMEM((WINDOW, WIDTH), F32)],
        compiler_params=_params("parallel", "arbitrary"),
        name="swa",
    )(sinks, proj)


def _merge_kernel(x_ref, gate_ref, yrw, yssm, yatt, gb, wrw, wssm, watt, wout, o_ref):
    merged = None
    for i, (y, w) in enumerate(((yrw, wrw), (yssm, wssm), (yatt, watt))):
        cols = slice(i * D_MODEL, (i + 1) * D_MODEL)
        gate = _sigmoid(gate_ref[:, cols].astype(F32) + gb[:, cols])
        t = gate * jnp.dot(y[...], w[...], preferred_element_type=F32)
        merged = t if merged is None else merged + t
    o_ref[...] = x_ref[...] + _dot(merged, wout[...])


def merge_out(x, proj, y_rw, y_ssm, y_att, prm, l, tm=512):
    n = x.shape[0]
    names = ["gate_b", "w_br_rw", "w_br_ssm", "w_br_att", "w_out"]
    specs, args = _layer_specs(prm, names, l)
    rows = lambda width: pl.BlockSpec((tm, width), lambda i: (i, 0))
    return pl.pallas_call(
        _merge_kernel,
        out_shape=jax.ShapeDtypeStruct((n, D_MODEL), F32),
        grid=(n // tm,),
        in_specs=[rows(D_MODEL), rows(3 * D_MODEL), rows(WIDTH), rows(WIDTH), rows(WIDTH)] + specs,
        out_specs=rows(D_MODEL),
        compiler_params=_params("parallel"),
        name="merge_out",
    )(x, proj, y_rw, y_ssm, y_att, *args)


def _ffn_kernel(x_ref, gamma, wgu, wd, gfin, o_ref, act, *, chunk, final_norm):
    x = x_ref[...]
    xn = _rms(x, gamma[...]).astype(BF16)
    for c0 in range(0, FFN_HIDDEN, chunk):
        gate = jnp.dot(xn, wgu[:, c0:c0 + chunk], preferred_element_type=F32)
        up = jnp.dot(xn, wgu[:, FFN_HIDDEN + c0:FFN_HIDDEN + c0 + chunk], preferred_element_type=F32)
        act[:, c0:c0 + chunk] = (gate * _sigmoid(gate) * up).astype(BF16)
    out = x + jnp.dot(act[...], wd[...], preferred_element_type=F32)
    if final_norm:
        out = _rms(out, gfin[...])
    o_ref[...] = out


def ffn(x, prm, l, final_norm, tm=512, chunk=256):
    n = x.shape[0]
    specs, args = _layer_specs(prm, ["norm_ffn", "w_gu", "w_down"], l)
    fspecs, fargs = _layer_specs(prm, ["norm_final"], 0)
    return pl.pallas_call(
        functools.partial(_ffn_kernel, chunk=chunk, final_norm=final_norm),
        out_shape=jax.ShapeDtypeStruct((n, D_MODEL), F32),
        grid=(n // tm,),
        in_specs=[pl.BlockSpec((tm, D_MODEL), lambda i: (i, 0))] + specs + fspecs,
        out_specs=pl.BlockSpec((tm, D_MODEL), lambda i: (i, 0)),
        scratch_shapes=[pltpu.VMEM((tm, FFN_HIDDEN), BF16)],
        compiler_params=_params("parallel"),
        name="ffn_final" if final_norm else "ffn",
    )(x, *args, *fargs)


def _pad_axis(w, size, axis, at=0):
    pads = [(0, 0)] * w.ndim
    pads[axis] = (at, size - at - w.shape[axis])
    return jnp.pad(w, pads)


def _prep_params(p):
    depth = p["w_in"].shape[0]
    rw_cols = 3 * WIDTH + RW_DECAY_LORA + RW_ICLR_LORA + RW_GATE_LORA
    ssm_cols = WIDTH + (WIDTH + 2 * SSM_GROUPS * SSM_STATE) + N_HEADS
    att_cols = WIDTH + 2 * ATT_KV_HEADS * HEAD_DIM
    o_ssm = rw_cols
    o_att = o_ssm + ssm_cols
    o_gate = o_att + att_cols
    o_k = o_att + WIDTH
    o_v = o_k + ATT_KV_HEADS * HEAD_DIM
    w = p["w_in"]
    zeros = lambda c: jnp.zeros((depth, D_MODEL, c), w.dtype)
    dup = lambda o: [w[:, :, o + HEAD_DIM * (i // 2):o + HEAD_DIM * (i // 2 + 1)] for i in range(4)]
    lv = jnp.concatenate([zeros(RW_VRES_LORA)[:1], p["rw_vres_down"]], axis=0)
    pad_lglv = 256 - RW_GATE_LORA - RW_VRES_LORA
    w_cat = jnp.concatenate(
        [w[:, :, o_gate:o_gate + 3 * D_MODEL],
         w[:, :, :3 * WIDTH],
         w[:, :, o_ssm:o_ssm + 3 * WIDTH],
         w[:, :, o_att:o_att + WIDTH]] + dup(o_k) + dup(o_v)
        + [w[:, :, 3 * WIDTH + 128:rw_cols], lv, zeros(pad_lglv),
           w[:, :, 3 * WIDTH:3 * WIDTH + 128],
           w[:, :, o_ssm + 3 * WIDTH:o_ssm + 3 * WIDTH + N_HEADS], zeros(128 - N_HEADS)], axis=2)
    assert w_cat.shape == (depth, D_MODEL, IN_COLS_PAD)

    row = lambda v: v.reshape(v.shape[0], 1, -1).astype(F32)
    mu = p["rw_mu"]
    vmu = jnp.concatenate([jnp.zeros((1, RW_VRES_LORA), F32), p["rw_vres_mu"]], axis=0)
    head_of = jnp.arange(WIDTH) // HEAD_DIM
    return {
        "norm_mix": row(p["norm_mix"]),
        "w_cat": w_cat.astype(BF16),
        "mu_rkv": row(mu[:, :3 * WIDTH]),
        "mu_lwla": row(mu[:, 3 * WIDTH:3 * WIDTH + 128]),
        "mu_lglv": row(jnp.concatenate([mu[:, 3 * WIDTH + 128:], vmu,
                                        jnp.zeros((depth, pad_lglv), F32)], axis=1)),
        "w0": row(p["rw_w0"]),
        "w2p": _pad_axis(p["rw_w2"], 128, 1, 0).astype(BF16),
        "a0": row(p["rw_a0"]),
        "a2p": _pad_axis(p["rw_a2"], 128, 1, RW_DECAY_LORA).astype(BF16),
        "g2p": _pad_axis(p["rw_g2"], 256, 1, 0).astype(BF16),
        "kkw": row(p["rw_k_k"]), "kaw": row(p["rw_k_a"]), "rkw": row(p["rw_r_k"]),
        "gnw": row(p["rw_gn_w"]), "gnb": row(p["rw_gn_b"]),
        "v0": row(p["rw_vres_v0"]),
        "v2p": _pad_axis(p["rw_vres_v2"], 256, 1, RW_GATE_LORA).astype(BF16),
        "hsum": (head_of[:, None] == head_of[None, :]).astype(BF16)[None],
        "convw": p["ssm_conv_w"].astype(F32),
        "convb": row(p["ssm_conv_b"]),
        "dtb": row(_pad_axis(p["ssm_dt_bias"], 128, 1)),
        "a_neg": row(_pad_axis(-jnp.exp(p["ssm_a_log"].astype(F32)), 128, 1)),
        "dskip": row(jnp.repeat(p["ssm_d"], HEAD_DIM, axis=1)),
        "normw": row(p["ssm_norm_w"]),
        "exp_ch": (jnp.arange(128)[:, None] == head_of[None, :]).astype(BF16)[None],
        "exp_ln": (jnp.arange(128)[:, None]
                   == (jnp.arange(N_HEADS * LANES) // LANES)[None, :]).astype(BF16)[None],
        "sinks": p["att_sinks"].astype(F32),
        "gate_b": row(p["gate_b"]),
        "w_br_rw": p["w_br_rw"].astype(BF16),
        "w_br_ssm": p["w_br_ssm"].astype(BF16),
        "w_br_att": p["w_br_att"].astype(BF16),
        "w_out": p["w_out"].astype(BF16),
        "norm_ffn": row(p["norm_ffn"]),
        "w_gu": p["ffn_w_gu"].astype(BF16),
        "w_down": p["ffn_w_down"].astype(BF16),
        "norm_final": p["norm_final"].reshape(1, 1, -1).astype(F32),
    }


def kernel(x, norm_mix, w_in, rw_mu, rw_w0, rw_w2, rw_a0, rw_a2, rw_g2, rw_k_k, rw_k_a, rw_r_k, rw_gn_w, rw_gn_b, rw_vres_down, rw_vres_mu, rw_vres_v0, rw_vres_v2, ssm_conv_w, ssm_conv_b, ssm_dt_bias, ssm_a_log, ssm_d, ssm_norm_w, att_sinks, gate_b, w_br_rw, w_br_ssm, w_br_att, w_out, norm_ffn, ffn_w_gu, ffn_w_down, norm_final):
    p = dict(norm_mix=norm_mix, w_in=w_in, rw_mu=rw_mu, rw_w0=rw_w0, rw_w2=rw_w2, rw_a0=rw_a0,
             rw_a2=rw_a2, rw_g2=rw_g2, rw_k_k=rw_k_k, rw_k_a=rw_k_a,
             rw_r_k=rw_r_k.reshape(rw_r_k.shape[0], -1), rw_gn_w=rw_gn_w, rw_gn_b=rw_gn_b,
             rw_vres_down=rw_vres_down, rw_vres_mu=rw_vres_mu, rw_vres_v0=rw_vres_v0,
             rw_vres_v2=rw_vres_v2, ssm_conv_w=ssm_conv_w, ssm_conv_b=ssm_conv_b,
             ssm_dt_bias=ssm_dt_bias, ssm_a_log=ssm_a_log, ssm_d=ssm_d, ssm_norm_w=ssm_norm_w,
             att_sinks=att_sinks, gate_b=gate_b, w_br_rw=w_br_rw, w_br_ssm=w_br_ssm,
             w_br_att=w_br_att, w_out=w_out, norm_ffn=norm_ffn, ffn_w_gu=ffn_w_gu,
             ffn_w_down=ffn_w_down, norm_final=norm_final)
    batch, seq, _ = x.shape
    depth = w_in.shape[0]
    prm = _prep_params(p)
    xf = x.reshape(batch * seq, D_MODEL)
    vfirst = None
    for l in range(depth):
        proj = in_proj(xf, prm, l)
        if l == 0:
            y_rw, vfirst = rwkv_mixer(proj, None, prm, l, batch, seq)
        else:
            y_rw = rwkv_mixer(proj, vfirst, prm, l, batch, seq)
        y_ssm = ssd_mixer(proj, prm, l, batch, seq)
        y_att = swa_mixer(proj, prm["sinks"], l, batch, seq)
        xf = merge_out(xf, proj, y_rw, y_ssm, y_att, prm, l)
        xf = ffn(xf, prm, l, final_norm=(l == depth - 1))
    return xf.reshape(batch, seq, D_MODEL)
```

```python
import functools

import jax
import jax.numpy as jnp
from jax import lax
from jax.experimental import pallas as pl
from jax.experimental.pallas import tpu as pltpu

F32 = jnp.float32
BF16 = jnp.bfloat16

D_MODEL = 1024
HEAD_DIM = 64
N_HEADS = 8
WIDTH = N_HEADS * HEAD_DIM
LANES = 128
N_PAIRS = WIDTH // LANES
RW_CHUNK = 64
RW_DECAY_LORA = 64
RW_ICLR_LORA = 64
RW_VRES_LORA = 32
RW_GATE_LORA = 160
RW_GN_EPS = 64e-5
SSM_STATE = 128
SSM_GROUPS = 2
SSM_CHUNK = 128
SSM_CONV = 4
ATT_KV_HEADS = 2
WINDOW = 128
FFN_HIDDEN = 2816
NORM_EPS = 1e-6
VMEM_LIMIT = 56 * 1024 * 1024

C_GATE = 0
C_RKV = 3072
C_SSM = 4608
C_ATT = 6144
C_LGLV = 7168
C_LWLA = 7424
C_DT = 7552
IN_COLS_PAD = 7680


def _dot(a, b):
    return jnp.dot(a.astype(BF16), b.astype(BF16), preferred_element_type=F32)


def _dot_nt(a, b):
    return lax.dot_general(a.astype(BF16), b.astype(BF16), (((1,), (1,)), ((), ())),
                           preferred_element_type=F32)


def _split3(x):
    hi = x.astype(BF16)
    r1 = x - hi.astype(F32)
    mid = r1.astype(BF16)
    return hi, mid, (r1 - mid.astype(F32)).astype(BF16)


def _dot01_left(m01, x):
    return sum(jnp.dot(m01, part, preferred_element_type=F32) for part in _split3(x))


def _dot01_right(x, m01):
    return sum(jnp.dot(part, m01, preferred_element_type=F32) for part in _split3(x))


def _dot_split(a, b01):
    hi = a.astype(BF16)
    lo = (a - hi.astype(F32)).astype(BF16)
    return (jnp.dot(hi, b01, preferred_element_type=F32)
            + jnp.dot(lo, b01, preferred_element_type=F32))


def _sigmoid(x):
    return 1.0 / (1.0 + jnp.exp(-x))


def _softplus(x):
    return jnp.maximum(x, 0.0) + jnp.log(1.0 + jnp.exp(-jnp.abs(x)))


def _rms(x, w):
    return x * lax.rsqrt(jnp.mean(x * x, axis=-1, keepdims=True) + NORM_EPS) * w


def _shift_rows(carry8, p, j):
    ext = jnp.concatenate([carry8, p], axis=0)
    return pltpu.roll(ext, j, 0)[8:]


def _params(*sem):
    return pltpu.CompilerParams(dimension_semantics=sem, vmem_limit_bytes=VMEM_LIMIT)


_SHARED = ("hsum", "exp_ch", "exp_ln")
_FROM_SECOND_LAYER = ("v0", "v2p")


def _layer_specs(prm, names, l):
    specs, args = [], []
    for nm in names:
        arr = prm[nm]
        idx = 0 if nm in _SHARED else (l - 1 if nm in _FROM_SECOND_LAYER else l)
        nd = arr.ndim
        specs.append(pl.BlockSpec((None,) + arr.shape[1:],
                                  lambda *_, idx=idx, nd=nd: (idx,) + (0,) * (nd - 1),
                                  pipeline_mode=pl.Buffered(1)))
        args.append(arr)
    return specs, args


def _in_proj_kernel(x_ref, g_ref, w_ref, o_ref, *, chunk):
    xn = _rms(x_ref[...], g_ref[...]).astype(BF16)
    for c0 in range(0, IN_COLS_PAD, chunk):
        o_ref[:, c0:c0 + chunk] = jnp.dot(
            xn, w_ref[:, c0:c0 + chunk], preferred_element_type=F32).astype(BF16)


def in_proj(x, prm, l, tm=512, chunk=512):
    n = x.shape[0]
    specs, args = _layer_specs(prm, ["norm_mix", "w_cat"], l)
    return pl.pallas_call(
        functools.partial(_in_proj_kernel, chunk=chunk),
        out_shape=jax.ShapeDtypeStruct((n, IN_COLS_PAD), BF16),
        grid=(n // tm,),
        in_specs=[pl.BlockSpec((tm, D_MODEL), lambda i: (i, 0))] + specs,
        out_specs=pl.BlockSpec((tm, IN_COLS_PAD), lambda i: (i, 0)),
        compiler_params=_params("parallel"),
        name="in_proj",
    )(x, *args)


def _split_bf16(x):
    hi = x.astype(BF16)
    return hi, (x - hi.astype(F32)).astype(BF16)


def _pair_tiles(xb, bd):
    return [jnp.concatenate([xb[:, i:i + LANES]] * 2, axis=0) * bd
            for i in range(0, xb.shape[1], LANES)]


def _mm_tiles(lhs, tiles):
    stacked = lhs[0] if len(lhs) == 1 else jnp.concatenate(lhs, axis=0)
    return jnp.concatenate(
        [jnp.dot(stacked[:, p * LANES:(p + 1) * LANES], t, preferred_element_type=F32)
         for p, t in enumerate(tiles)], axis=1)


def _mm_split(a_hl, b_tiles_hl, m):
    r1 = _mm_tiles(list(a_hl), b_tiles_hl[0])
    return r1[:m] + r1[m:] + _mm_tiles([a_hl[0]], b_tiles_hl[1])


def _headwise_mm(a, b, bd):
    return _mm_tiles([a.astype(BF16)], _pair_tiles(b.astype(BF16), bd))


def _tri_inverse(n_cat, bd):
    size, width = n_cat.shape
    row = lax.broadcasted_iota(jnp.int32, (size, LANES), 0)
    col = lax.broadcasted_iota(jnp.int32, (size, LANES), 1) % size
    wide = lambda tile: jnp.concatenate([tile] * (width // LANES), axis=1)

    def same_block(b):
        return (row // b) == (col // b)

    a0 = jnp.where(wide(same_block(8)), n_cat, 0.0).astype(BF16)
    d0 = wide(jnp.where(row == col, 1.0, 0.0).astype(BF16)) + a0
    p = _mm_tiles([a0], _pair_tiles(a0, bd))
    p_hl = _split_bf16(p)
    p_tiles = (_pair_tiles(p_hl[0], bd), _pair_tiles(p_hl[1], bd))
    r1 = _mm_tiles([d0, p_hl[0], p_hl[1]], p_tiles[0])
    r2 = _mm_tiles([d0, p_hl[0]], p_tiles[1])
    d = d0.astype(F32) + r1[:size] + r2[:size]
    p = r1[size:2 * size] + r1[2 * size:] + r2[size:]
    p_hl = _split_bf16(p)
    d = d + _mm_split(_split_bf16(d), (_pair_tiles(p_hl[0], bd), _pair_tiles(p_hl[1], bd)), size)
    b = 8
    while b < size:
        e = jnp.where(wide(same_block(2 * b) & jnp.logical_not(same_block(b))), n_cat, 0.0).astype(BF16)
        d_hl = _split_bf16(d)
        r = _mm_tiles(list(d_hl), _pair_tiles(e, bd))
        x_hl = _split_bf16(r[:size] + r[size:])
        d = d + _mm_split(x_hl, (_pair_tiles(d_hl[0], bd), _pair_tiles(d_hl[1], bd)), size)
        b *= 2
    return d


def _rwkv_kernel(*refs, has_vres, tb, group):
    if has_vres:
        (rkv_ref, lglv_ref, lwla_ref, vfirst_ref, mu_rkv, mu_lglv, mu_lwla, w0, w2p, a0, a2p,
         g2p, kkw, kaw, rkw, gnw, gnb, hsum, v0, v2p,
         y_ref,
         c_rkv, c_lglv, c_lwla, s_ref, phi_s, psi_s, etot_s,
         r_s, k_s, v_s, kk_s, kka_s, lw_s, g_s, y_s, rhat_s, y0_s) = refs
    else:
        (rkv_ref, lglv_ref, lwla_ref, mu_rkv, mu_lglv, mu_lwla, w0, w2p, a0, a2p,
         g2p, kkw, kaw, rkw, gnw, gnb, hsum,
         y_ref, vfirst_out,
         c_rkv, c_lglv, c_lwla, s_ref, phi_s, psi_s, etot_s,
         r_s, k_s, v_s, kk_s, kka_s, lw_s, g_s, y_s, rhat_s, y0_s) = refs

    @pl.when(pl.program_id(1) == 0)
    def _():
        c_rkv[...] = jnp.zeros_like(c_rkv)
        c_lglv[...] = jnp.zeros_like(c_lglv)
        c_lwla[...] = jnp.zeros_like(c_lwla)
        s_ref[...] = jnp.zeros_like(s_ref)

    def mix(p_ref, carry, mu):
        p = p_ref[...].astype(F32)
        prev = _shift_rows(carry[...], p, 1)
        carry[...] = p[tb - 8:]
        return p + (prev - p) * mu[...]

    rkv = mix(rkv_ref, c_rkv, mu_rkv)
    lglv = mix(lglv_ref, c_lglv, mu_lglv)
    lwla = mix(lwla_ref, c_lwla, mu_lwla)
    r = rkv[:, :WIDTH]
    k = rkv[:, WIDTH:2 * WIDTH]
    v = rkv[:, 2 * WIDTH:]

    w = -_softplus(-(w0[...] + _dot(jnp.tanh(lwla), w2p[...]))) - 0.5
    lw_s[...] = -jnp.exp(w)
    a = _sigmoid(a0[...] + _dot(lwla, a2p[...]))
    g_s[...] = _dot(_sigmoid(lglv), g2p[...])
    if has_vres:
        v = v + (vfirst_ref[...].astype(F32) - v) * _sigmoid(v0[...] + _dot(lglv, v2p[...]))
    else:
        vfirst_out[...] = v.astype(BF16)
    hs = hsum[...]
    kk = k * kkw[...]
    kk = kk * lax.rsqrt(jnp.maximum(_dot(kk * kk, hs), 1e-24))
    k = k * (1.0 + (a - 1.0) * kaw[...])
    r_s[...] = r
    k_s[...] = k
    v_s[...] = v
    kk_s[...] = kk
    kka_s[...] = kk * a

    cl = RW_CHUNK
    n_chunks = tb // cl
    row = lax.broadcasted_iota(jnp.int32, (cl, cl), 0)
    col = lax.broadcasted_iota(jnp.int32, (cl, cl), 1)
    tri_incl = jnp.where(row >= col, 1.0, 0.0).astype(BF16)
    row2 = lax.broadcasted_iota(jnp.int32, (2 * cl, 2 * LANES), 0)
    col2 = lax.broadcasted_iota(jnp.int32, (2 * cl, 2 * LANES), 1)
    m_mask = (row2 % cl - col2 % cl) >= jnp.where(row2 < cl, 1, 0)
    rr = lax.broadcasted_iota(jnp.int32, (LANES, LANES), 0)
    cc = lax.broadcasted_iota(jnp.int32, (LANES, LANES), 1)
    same_head = (rr // HEAD_DIM) == (cc // HEAD_DIM)
    lane_lo = cc < HEAD_DIM
    bd = jnp.where(same_head, 1.0, 0.0).astype(BF16)
    same_head_w = jnp.concatenate([same_head, same_head], axis=1)
    same_head_t = jnp.concatenate([same_head, same_head], axis=0)
    zeros_half = jnp.zeros((cl, LANES), F32)

    def group_pre(gi, carry):
        chunks = [gi * group + g for g in range(group)]
        rows = [pl.ds(pl.multiple_of(ci * cl, cl), cl) for ci in chunks]
        lws = [lw_s[r, :] for r in rows]
        cums = [_dot01_left(tri_incl, lw) for lw in lws]
        rts, ats, bts, kts, e_mids, e_ends = [], [], [], [], [], []
        for g in range(group):
            cum, lw = cums[g], lws[g]
            mid = cum[cl // 2 - 1:cl // 2, :]
            cen = cum - mid
            e_pos = jnp.exp(cen)
            e_neg = jnp.exp(-cen)
            rts.append(r_s[rows[g], :] * e_pos)
            ats.append(-kk_s[rows[g], :] * jnp.exp(cen - lw))
            bts.append(kka_s[rows[g], :] * e_neg)
            kts.append(k_s[rows[g], :] * e_neg)
            e_mids.append(jnp.exp(mid))
            e_ends.append(jnp.exp(cen[cl - 1:cl, :]))
            etot_s[chunks[g]] = jnp.broadcast_to(jnp.exp(cum[cl - 1:cl, :]), (8, WIDTH))
        pairs = [(g, p) for g in range(group) for p in range(N_PAIRS)]
        lane = lambda p: slice(p * LANES, (p + 1) * LANES)
        zs = [jnp.concatenate([bts[g][:, lane(p)], kts[g][:, lane(p)]], axis=0)
              for g, p in pairs]
        zts = [z.T for z in zs]
        ms = []
        for (g, p), zt in zip(pairs, zts):
            zr = pltpu.roll(zt, HEAD_DIM, 1)
            wgt = jnp.concatenate([jnp.where(lane_lo, zt, zr), jnp.where(lane_lo, zr, zt)], axis=1)
            wgt = jnp.where(same_head_w, wgt, 0.0)
            m = _dot(jnp.concatenate([ats[g][:, lane(p)], rts[g][:, lane(p)]], axis=0), wgt)
            ms.append(jnp.where(m_mask, m, 0.0))
        rt = jnp.concatenate(rts, axis=1)
        at = jnp.concatenate(ats, axis=1)
        vv = jnp.concatenate([v_s[r, :] for r in rows], axis=1)
        e_mid = jnp.concatenate(e_mids, axis=1)
        a_ab = jnp.concatenate([m[:cl, :LANES] for m in ms], axis=1)
        a_kr = jnp.concatenate(
            [jnp.concatenate([m[:cl, LANES:] for m in ms], axis=1),
             jnp.concatenate([m[cl:, LANES:] for m in ms], axis=1)], axis=0)
        a_rb = jnp.concatenate([m[cl:, :LANES] for m in ms], axis=1).astype(BF16)
        t_hl = _split_bf16(_tri_inverse(a_ab, bd))
        akv = _headwise_mm(a_kr, vv, bd)
        at_hl = _split_bf16(at)
        av_hl = _split_bf16(akv[:cl])
        at_t = _mm_split(t_hl, (_pair_tiles(at_hl[0], bd), _pair_tiles(at_hl[1], bd)), cl)
        u0 = _mm_split(t_hl, (_pair_tiles(av_hl[0], bd), _pair_tiles(av_hl[1], bd)), cl)
        rhat = (rt + _mm_tiles([a_rb], _pair_tiles(at_t.astype(BF16), bd))) * e_mid
        y0 = _mm_tiles([a_rb], _pair_tiles(u0.astype(BF16), bd)) + akv[cl:]
        at_m = at_t * e_mid
        for g in range(group):
            rhat_s[rows[g], :] = rhat[:, g * WIDTH:(g + 1) * WIDTH]
            y0_s[rows[g], :] = y0[:, g * WIDTH:(g + 1) * WIDTH]
        for i, (g, p) in enumerate(pairs):
            ln = slice(g * WIDTH + p * LANES, g * WIDTH + (p + 1) * LANES)
            lhs = jnp.concatenate(
                [jnp.concatenate([at_m[:, ln], zeros_half], axis=0),
                 jnp.concatenate([u0[:, ln], vv[:, ln]], axis=0)], axis=1)
            pp = _dot(lhs.T, zs[i]) * e_ends[g][:, lane(p)]
            pp = jnp.where(same_head_t, pp, 0.0)
            phi_s[chunks[g], p] = pp[:LANES]
            psi_s[chunks[g], p] = pp[LANES:]
        return carry

    if n_chunks == group:
        group_pre(0, 0)
    else:
        lax.fori_loop(0, n_chunks // group, group_pre, 0)

    states = [s_ref[p] for p in range(N_PAIRS)]
    for ci in range(n_chunks):
        rows = slice(ci * cl, (ci + 1) * cl)
        et = etot_s[ci][0:1]
        for p in range(N_PAIRS):
            ln = slice(p * LANES, (p + 1) * LANES)
            s0 = states[p]
            y_s[rows, ln] = _dot_nt(rhat_s[rows, ln], s0) + y0_s[rows, ln]
            states[p] = s0 * et[:, ln] + _dot(s0, phi_s[ci, p]) + psi_s[ci, p]
    for p in range(N_PAIRS):
        s_ref[p] = states[p]

    y = y_s[...]
    inv_n = 1.0 / HEAD_DIM
    mean = _dot(y, hs) * inv_n
    yc = y - mean
    var = _dot(yc * yc, hs) * inv_n
    yn = yc * lax.rsqrt(var + RW_GN_EPS) * gnw[...] + gnb[...]
    bonus = _dot(r_s[...] * k_s[...] * rkw[...], hs) * v_s[...]
    y_ref[...] = ((yn + bonus) * g_s[...]).astype(BF16)


def rwkv_mixer(proj, vfirst, prm, l, batch, seq, tb=512, group=8):
    n = proj.shape[0]
    nt = seq // tb
    has_vres = vfirst is not None
    row = lambda b, t: b * nt + t
    in_specs = [pl.BlockSpec((tb, 3 * WIDTH), lambda b, t: (row(b, t), C_RKV // (3 * WIDTH))),
                pl.BlockSpec((tb, 256), lambda b, t: (row(b, t), C_LGLV // 256)),
                pl.BlockSpec((tb, 128), lambda b, t: (row(b, t), C_LWLA // 128))]
    args = [proj, proj, proj]
    if has_vres:
        in_specs.append(pl.BlockSpec((tb, WIDTH), lambda b, t: (row(b, t), 0)))
        args.append(vfirst)
    names = ["mu_rkv", "mu_lglv", "mu_lwla", "w0", "w2p", "a0", "a2p", "g2p", "kkw", "kaw", "rkw",
             "gnw", "gnb", "hsum"]
    if has_vres:
        names += ["v0", "v2p"]
    specs, pargs = _layer_specs(prm, names, l)
    in_specs += specs
    args += pargs
    y_spec = pl.BlockSpec((tb, WIDTH), lambda b, t: (row(b, t), 0))
    y_shape = jax.ShapeDtypeStruct((n, WIDTH), BF16)
    if has_vres:
        out_shape, out_specs = y_shape, y_spec
    else:
        out_shape, out_specs = (y_shape, y_shape), (y_spec, y_spec)
    big = pltpu.VMEM((tb, WIDTH), F32)
    nc = tb // RW_CHUNK
    pair_mats = pltpu.VMEM((nc, N_PAIRS, LANES, LANES), F32)
    scratch = [pltpu.VMEM((8, 3 * WIDTH), F32), pltpu.VMEM((8, 256), F32), pltpu.VMEM((8, 128), F32),
               pltpu.VMEM((N_PAIRS, LANES, LANES), F32), pair_mats, pair_mats,
               pltpu.VMEM((nc, 8, WIDTH), F32)] + [big] * 10
    return pl.pallas_call(
        functools.partial(_rwkv_kernel, has_vres=has_vres, tb=tb, group=group),
        out_shape=out_shape, grid=(batch, nt), in_specs=in_specs, out_specs=out_specs,
        scratch_shapes=scratch, compiler_params=_params("parallel", "arbitrary"),
        name="rwkv7_vres" if has_vres else "rwkv7",
    )(*args)


def _ssd_kernel(zx_ref, dt_ref, convw, convb, dtb, a_neg, dskip, normw, exp_ch, exp_ln,
                y_ref, c_conv, st_ref, *, nb):
    q = SSM_CHUNK
    rows_all = nb * q

    @pl.when(pl.program_id(1) == 0)
    def _():
        c_conv[...] = jnp.zeros_like(c_conv)
        st_ref[...] = jnp.zeros_like(st_ref)

    zx = zx_ref[...].astype(F32)
    z_all = zx[:, :WIDTH]
    xbc = zx[:, WIDTH:]
    carry = c_conv[...]
    conv = xbc * convw[SSM_CONV - 1:SSM_CONV, :] + convb[...]
    for j in range(1, SSM_CONV):
        conv = conv + _shift_rows(carry, xbc, j) * convw[SSM_CONV - 1 - j:SSM_CONV - j, :]
    c_conv[...] = xbc[rows_all - 8:]
    xbc = conv * _sigmoid(conv)
    dt_all = _softplus(dt_ref[...].astype(F32) + dtb[...])
    a_all = dt_all * a_neg[...]
    row = lax.broadcasted_iota(jnp.int32, (q, q), 0)
    col = lax.broadcasted_iota(jnp.int32, (q, q), 1)
    causal = row >= col
    tri = jnp.where(causal, 1.0, 0.0).astype(BF16)
    ech = exp_ch[...]
    eln = exp_ln[...]
    lo = lax.broadcasted_iota(jnp.int32, (q, LANES), 1) < HEAD_DIM
    per_g = N_HEADS // SSM_GROUPS
    gw = per_g * HEAD_DIM
    states = [st_ref[g] for g in range(SSM_GROUPS)]
    for sb in range(nb):
        rs = slice(sb * q, (sb + 1) * q)
        xs = xbc[rs, :WIDTH]
        bm = xbc[rs, WIDTH:WIDTH + SSM_GROUPS * SSM_STATE]
        cm = xbc[rs, WIDTH + SSM_GROUPS * SSM_STATE:]
        dt = dt_all[rs]
        acum = _dot01_left(tri, a_all[rs])
        acum_t = acum.T
        dt_x = _dot_split(dt, ech)
        acum_x = _dot01_right(acum, ech)
        atot_x = acum_x[q - 1:q, :]
        acum_l = _dot01_right(acum, eln)
        xdt = xs * dt_x
        xdec = xdt * jnp.exp(atot_x - acum_x)
        e_in = jnp.exp(acum_x)
        e_tot = jnp.exp(atot_x)
        ys = []
        for g in range(SSM_GROUPS):
            bg = bm[:, g * SSM_STATE:(g + 1) * SSM_STATE]
            cg = cm[:, g * SSM_STATE:(g + 1) * SSM_STATE]
            cb = _dot_nt(cg, bg)
            gl = slice(g * gw, (g + 1) * gw)
            y_g = _dot(cg, states[g]) * e_in[:, gl]
            states[g] = states[g] * e_tot[:, gl] + _dot(bg.T, xdec[:, gl])
            diag = []
            for pp in range(per_g // 2):
                xp = xdt[:, gl][:, pp * LANES:(pp + 1) * LANES]
                acc = None
                for hh in range(2):
                    h = g * per_g + 2 * pp + hh
                    diff = acum_l[:, h * LANES:(h + 1) * LANES] - acum_t[h:h + 1, :]
                    lmat = jnp.where(causal, jnp.exp(jnp.where(causal, diff, 0.0)), 0.0)
                    xh = jnp.where(lo, xp, 0.0) if hh == 0 else jnp.where(lo, 0.0, xp)
                    t = _dot(cb * lmat, xh)
                    acc = t if acc is None else acc + t
                diag.append(acc)
            ys.append(y_g + jnp.concatenate(diag, axis=1))
        y = jnp.concatenate(ys, axis=1) + xs * dskip[...]
        z = z_all[rs]
        yz = y * (z * _sigmoid(z))
        outs = []
        for g in range(SSM_GROUPS):
            yg = yz[:, g * gw:(g + 1) * gw]
            outs.append(yg * lax.rsqrt(jnp.mean(yg * yg, axis=-1, keepdims=True) + NORM_EPS))
        y_ref[rs, :] = (jnp.concatenate(outs, axis=1) * normw[...]).astype(BF16)
    for g in range(SSM_GROUPS):
        st_ref[g] = states[g]


def ssd_mixer(proj, prm, l, batch, seq, nb=4):
    n = proj.shape[0]
    rows = nb * SSM_CHUNK
    nt = seq // rows
    row = lambda b, t: b * nt + t
    names = ["convw", "convb", "dtb", "a_neg", "dskip", "normw", "exp_ch", "exp_ln"]
    specs, args = _layer_specs(prm, names, l)
    in_specs = [pl.BlockSpec((rows, 3 * WIDTH), lambda b, t: (row(b, t), C_SSM // (3 * WIDTH))),
                pl.BlockSpec((rows, 128), lambda b, t: (row(b, t), C_DT // 128))] + specs
    return pl.pallas_call(
        functools.partial(_ssd_kernel, nb=nb),
        out_shape=jax.ShapeDtypeStruct((n, WIDTH), BF16),
        grid=(batch, nt), in_specs=in_specs,
        out_specs=pl.BlockSpec((rows, WIDTH), lambda b, t: (row(b, t), 0)),
        scratch_shapes=[pltpu.VMEM((8, WIDTH + 2 * SSM_GROUPS * SSM_STATE), F32),
                        pltpu.VMEM((SSM_GROUPS, SSM_STATE, WIDTH // SSM_GROUPS), F32)],
        compiler_params=_params("parallel", "arbitrary"),
        name="ssd",
    )(proj, proj, *args)


def _swa_kernel(sink_ref, qkv_ref, y_ref, kv_prev, *, nb, layer):
    wdw = WINDOW

    @pl.when(pl.program_id(1) == 0)
    def _():
        kv_prev[...] = jnp.zeros_like(kv_prev)

    qkv = qkv_ref[...].astype(F32)
    qi = lax.broadcasted_iota(jnp.int32, (wdw, 2 * wdw), 0)
    kj = lax.broadcasted_iota(jnp.int32, (wdw, 2 * wdw), 1)
    rel = qi + wdw - kj
    in_window = (rel >= 0) & (rel < wdw)
    first = jnp.where(pl.program_id(1) > 0, 0, wdw)
    lo = lax.broadcasted_iota(jnp.int32, (wdw, LANES), 1) < HEAD_DIM
    lo2 = lax.broadcasted_iota(jnp.int32, (2 * wdw, LANES), 1) < HEAD_DIM
    rep = N_HEADS // ATT_KV_HEADS
    scale = HEAD_DIM ** -0.5
    bands, scores = [], []
    prev = kv_prev[...]
    for sb in range(nb):
        rs = slice(sb * wdw, (sb + 1) * wdw)
        kv = qkv[rs, WIDTH:]
        band = jnp.concatenate([prev, kv], axis=0)
        prev = kv
        bands.append(band)
        valid = in_window & (kj >= first) if sb == 0 else in_window
        for h in range(N_HEADS):
            g = h // rep
            qp = qkv[rs, (h // 2) * LANES:(h // 2 + 1) * LANES]
            qh = jnp.where(lo, qp, 0.0) if h % 2 == 0 else jnp.where(lo, 0.0, qp)
            s = _dot_nt(qh, band[:, g * LANES:(g + 1) * LANES]) * scale
            scores.append(jnp.where(valid, s, -1e30))
    kv_prev[...] = prev
    probs = []
    for i, s in enumerate(scores):
        sink = sink_ref[layer, i % N_HEADS]
        m = jnp.maximum(jnp.max(s, axis=-1, keepdims=True), sink)
        p = jnp.exp(s - m)
        den = jnp.sum(p, axis=-1, keepdims=True) + jnp.exp(sink - m)
        probs.append(p / den)
    for sb in range(nb):
        outs = []
        for tile in range(N_HEADS // 2):
            g = (2 * tile) // rep
            vb = bands[sb][:, (ATT_KV_HEADS + g) * LANES:(ATT_KV_HEADS + g + 1) * LANES]
            outs.append(_dot(probs[sb * N_HEADS + 2 * tile], jnp.where(lo2, vb, 0.0))
                        + _dot(probs[sb * N_HEADS + 2 * tile + 1], jnp.where(lo2, 0.0, vb)))
        y_ref[sb * wdw:(sb + 1) * wdw, :] = jnp.concatenate(outs, axis=1).astype(BF16)


def swa_mixer(proj, sinks, l, batch, seq, nb=4):
    n = proj.shape[0]
    rows = nb * WINDOW
    nt = seq // rows
    row = lambda b, t: b * nt + t
    return pl.pallas_call(
        functools.partial(_swa_kernel, nb=nb, layer=l),
        out_shape=jax.ShapeDtypeStruct((n, WIDTH), BF16),
        grid=(batch, nt),
        in_specs=[pl.BlockSpec(memory_space=pltpu.SMEM),
                  pl.BlockSpec((rows, 2 * WIDTH), lambda b, t: (row(b, t), C_ATT // (2 * WIDTH)))],
        out_specs=pl.BlockSpec((rows, WIDTH), lambda b, t: (row(b, t), 0)),
        scratch_shapes=[pltpu.VMEM((WINDOW, WIDTH), F32)],
        compiler_params=_params("parallel", "arbitrary"),
        name="swa",
    )(sinks, proj)


def _merge_ffn_kernel(x_ref, gate_ref, yrw, yssm, yatt, gb, wrw, wssm, watt, wout, gamma, wgu, wd,
                      gfin, o_ref, act, *, chunk, final_norm):
    merged = None
    for i, (y, w) in enumerate(((yrw, wrw), (yssm, wssm), (yatt, watt))):
        cols = slice(i * D_MODEL, (i + 1) * D_MODEL)
        gate = _sigmoid(gate_ref[:, cols].astype(F32) + gb[:, cols])
        t = gate * jnp.dot(y[...], w[...], preferred_element_type=F32)
        merged = t if merged is None else merged + t
    x = x_ref[...] + _dot(merged, wout[...])
    xn = _rms(x, gamma[...]).astype(BF16)
    for c0 in range(0, FFN_HIDDEN, chunk):
        gate = jnp.dot(xn, wgu[:, c0:c0 + chunk], preferred_element_type=F32)
        up = jnp.dot(xn, wgu[:, FFN_HIDDEN + c0:FFN_HIDDEN + c0 + chunk], preferred_element_type=F32)
        act[:, c0:c0 + chunk] = (gate * _sigmoid(gate) * up).astype(BF16)
    out = x + jnp.dot(act[...], wd[...], preferred_element_type=F32)
    if final_norm:
        out = _rms(out, gfin[...])
    o_ref[...] = out


def merge_ffn(x, proj, y_rw, y_ssm, y_att, prm, l, final_norm, tm=512, chunk=256):
    n = x.shape[0]
    names = ["gate_b", "w_br_rw", "w_br_ssm", "w_br_att", "w_out", "norm_ffn", "w_gu", "w_down"]
    specs, args = _layer_specs(prm, names, l)
    fspecs, fargs = _layer_specs(prm, ["norm_final"], 0)
    rows = lambda width: pl.BlockSpec((tm, width), lambda i: (i, 0))
    return pl.pallas_call(
        functools.partial(_merge_ffn_kernel, chunk=chunk, final_norm=final_norm),
        out_shape=jax.ShapeDtypeStruct((n, D_MODEL), F32),
        grid=(n // tm,),
        in_specs=[rows(D_MODEL), rows(3 * D_MODEL), rows(WIDTH), rows(WIDTH), rows(WIDTH)]
        + specs + fspecs,
        out_specs=rows(D_MODEL),
        scratch_shapes=[pltpu.VMEM((tm, FFN_HIDDEN), BF16)],
        compiler_params=_params("parallel"),
        name="merge_ffn_final" if final_norm else "merge_ffn",
    )(x, proj, y_rw, y_ssm, y_att, *args, *fargs)


def _pad_axis(w, size, axis, at=0):
    pads = [(0, 0)] * w.ndim
    pads[axis] = (at, size - at - w.shape[axis])
    return jnp.pad(w, pads)


def _prep_params(p):
    depth = p["w_in"].shape[0]
    rw_cols = 3 * WIDTH + RW_DECAY_LORA + RW_ICLR_LORA + RW_GATE_LORA
    ssm_cols = WIDTH + (WIDTH + 2 * SSM_GROUPS * SSM_STATE) + N_HEADS
    att_cols = WIDTH + 2 * ATT_KV_HEADS * HEAD_DIM
    o_ssm = rw_cols
    o_att = o_ssm + ssm_cols
    o_gate = o_att + att_cols
    o_k = o_att + WIDTH
    o_v = o_k + ATT_KV_HEADS * HEAD_DIM
    w = p["w_in"].astype(BF16)
    zeros = lambda c: jnp.zeros((depth, D_MODEL, c), w.dtype)
    dup = lambda o: [w[:, :, o + HEAD_DIM * (i // 2):o + HEAD_DIM * (i // 2 + 1)] for i in range(4)]
    lv = jnp.concatenate([zeros(RW_VRES_LORA)[:1], p["rw_vres_down"].astype(BF16)], axis=0)
    pad_lglv = 256 - RW_GATE_LORA - RW_VRES_LORA
    w_cat = jnp.concatenate(
        [w[:, :, o_gate:o_gate + 3 * D_MODEL],
         w[:, :, :3 * WIDTH],
         w[:, :, o_ssm:o_ssm + 3 * WIDTH],
         w[:, :, o_att:o_att + WIDTH]] + dup(o_k) + dup(o_v)
        + [w[:, :, 3 * WIDTH + 128:rw_cols], lv, zeros(pad_lglv),
           w[:, :, 3 * WIDTH:3 * WIDTH + 128],
           w[:, :, o_ssm + 3 * WIDTH:o_ssm + 3 * WIDTH + N_HEADS], zeros(128 - N_HEADS)], axis=2)
    assert w_cat.shape == (depth, D_MODEL, IN_COLS_PAD)

    row = lambda v: v.reshape(v.shape[0], 1, -1).astype(F32)
    mu = p["rw_mu"]
    vmu = jnp.concatenate([jnp.zeros((1, RW_VRES_LORA), F32), p["rw_vres_mu"]], axis=0)
    head_of = jnp.arange(WIDTH) // HEAD_DIM
    return {
        "norm_mix": row(p["norm_mix"]),
        "w_cat": w_cat,
        "mu_rkv": row(mu[:, :3 * WIDTH]),
        "mu_lwla": row(mu[:, 3 * WIDTH:3 * WIDTH + 128]),
        "mu_lglv": row(jnp.concatenate([mu[:, 3 * WIDTH + 128:], vmu,
                                        jnp.zeros((depth, pad_lglv), F32)], axis=1)),
        "w0": row(p["rw_w0"]),
        "w2p": _pad_axis(p["rw_w2"], 128, 1, 0).astype(BF16),
        "a0": row(p["rw_a0"]),
        "a2p": _pad_axis(p["rw_a2"], 128, 1, RW_DECAY_LORA).astype(BF16),
        "g2p": _pad_axis(p["rw_g2"], 256, 1, 0).astype(BF16),
        "kkw": row(p["rw_k_k"]), "kaw": row(p["rw_k_a"]), "rkw": row(p["rw_r_k"]),
        "gnw": row(p["rw_gn_w"]), "gnb": row(p["rw_gn_b"]),
        "v0": row(p["rw_vres_v0"]),
        "v2p": _pad_axis(p["rw_vres_v2"], 256, 1, RW_GATE_LORA).astype(BF16),
        "hsum": (head_of[:, None] == head_of[None, :]).astype(BF16)[None],
        "convw": p["ssm_conv_w"].astype(F32),
        "convb": row(p["ssm_conv_b"]),
        "dtb": row(_pad_axis(p["ssm_dt_bias"], 128, 1)),
        "a_neg": row(_pad_axis(-jnp.exp(p["ssm_a_log"].astype(F32)), 128, 1)),
        "dskip": row(jnp.repeat(p["ssm_d"], HEAD_DIM, axis=1)),
        "normw": row(p["ssm_norm_w"]),
        "exp_ch": (jnp.arange(128)[:, None] == head_of[None, :]).astype(BF16)[None],
        "exp_ln": (jnp.arange(128)[:, None]
                   == (jnp.arange(N_HEADS * LANES) // LANES)[None, :]).astype(BF16)[None],
        "sinks": p["att_sinks"].astype(F32),
        "gate_b": row(p["gate_b"]),
        "w_br_rw": p["w_br_rw"].astype(BF16),
        "w_br_ssm": p["w_br_ssm"].astype(BF16),
        "w_br_att": p["w_br_att"].astype(BF16),
        "w_out": p["w_out"].astype(BF16),
        "norm_ffn": row(p["norm_ffn"]),
        "w_gu": p["ffn_w_gu"].astype(BF16),
        "w_down": p["ffn_w_down"].astype(BF16),
        "norm_final": p["norm_final"].reshape(1, 1, -1).astype(F32),
    }


def kernel(x, norm_mix, w_in, rw_mu, rw_w0, rw_w2, rw_a0, rw_a2, rw_g2, rw_k_k, rw_k_a, rw_r_k, rw_gn_w, rw_gn_b, rw_vres_down, rw_vres_mu, rw_vres_v0, rw_vres_v2, ssm_conv_w, ssm_conv_b, ssm_dt_bias, ssm_a_log, ssm_d, ssm_norm_w, att_sinks, gate_b, w_br_rw, w_br_ssm, w_br_att, w_out, norm_ffn, ffn_w_gu, ffn_w_down, norm_final):
    p = dict(norm_mix=norm_mix, w_in=w_in, rw_mu=rw_mu, rw_w0=rw_w0, rw_w2=rw_w2, rw_a0=rw_a0,
             rw_a2=rw_a2, rw_g2=rw_g2, rw_k_k=rw_k_k, rw_k_a=rw_k_a,
             rw_r_k=rw_r_k.reshape(rw_r_k.shape[0], -1), rw_gn_w=rw_gn_w, rw_gn_b=rw_gn_b,
             rw_vres_down=rw_vres_down, rw_vres_mu=rw_vres_mu, rw_vres_v0=rw_vres_v0,
             rw_vres_v2=rw_vres_v2, ssm_conv_w=ssm_conv_w, ssm_conv_b=ssm_conv_b,
             ssm_dt_bias=ssm_dt_bias, ssm_a_log=ssm_a_log, ssm_d=ssm_d, ssm_norm_w=ssm_norm_w,
             att_sinks=att_sinks, gate_b=gate_b, w_br_rw=w_br_rw, w_br_ssm=w_br_ssm,
             w_br_att=w_br_att, w_out=w_out, norm_ffn=norm_ffn, ffn_w_gu=ffn_w_gu,
             ffn_w_down=ffn_w_down, norm_final=norm_final)
    batch, seq, _ = x.shape
    depth = w_in.shape[0]
    prm = _prep_params(p)
    xf = x.reshape(batch * seq, D_MODEL)
    vfirst = None
    for l in range(depth):
        proj = in_proj(xf, prm, l)
        if l == 0:
            y_rw, vfirst = rwkv_mixer(proj, None, prm, l, batch, seq)
        else:
            y_rw = rwkv_mixer(proj, vfirst, prm, l, batch, seq)
        y_ssm = ssd_mixer(proj, prm, l, batch, seq)
        y_att = swa_mixer(proj, prm["sinks"], l, batch, seq)
        xf = merge_ffn(xf, proj, y_rw, y_ssm, y_att, prm, l, final_norm=(l == depth - 1))
    return xf.reshape(batch, seq, D_MODEL)
```

```python
import functools

import jax
import jax.numpy as jnp
from jax import lax
from jax.experimental import pallas as pl
from jax.experimental.pallas import tpu as pltpu

F32 = jnp.float32
BF16 = jnp.bfloat16

D_MODEL = 1024
HEAD_DIM = 64
N_HEADS = 8
WIDTH = N_HEADS * HEAD_DIM
LANES = 128
N_PAIRS = WIDTH // LANES
RW_CHUNK = 64
RW_DECAY_LORA = 64
RW_ICLR_LORA = 64
RW_VRES_LORA = 32
RW_GATE_LORA = 160
RW_GN_EPS = 64e-5
RW_DECAY_SCALE = 0.6065306597126334
SSM_STATE = 128
SSM_GROUPS = 2
SSM_CHUNK = 128
SSM_CONV = 4
ATT_KV_HEADS = 2
WINDOW = 128
FFN_HIDDEN = 2816
NORM_EPS = 1e-6
VMEM_LIMIT = 56 * 1024 * 1024

C_GATE = 0
C_RKV = 3072
C_SSM = 4608
C_ATT = 6144
C_LGLV = 7168
C_LWLA = 7424
C_DT = 7552
IN_COLS_PAD = 7680


def _dot(a, b):
    return jnp.dot(a.astype(BF16), b.astype(BF16), preferred_element_type=F32)


def _dot_nt(a, b):
    return lax.dot_general(a.astype(BF16), b.astype(BF16), (((1,), (1,)), ((), ())),
                           preferred_element_type=F32)


def _split3(x):
    hi = x.astype(BF16)
    r1 = x - hi.astype(F32)
    mid = r1.astype(BF16)
    return hi, mid, (r1 - mid.astype(F32)).astype(BF16)


def _dot01_left(m01, x):
    return sum(jnp.dot(m01, part, preferred_element_type=F32) for part in _split3(x))


def _dot01_right(x, m01):
    return sum(jnp.dot(part, m01, preferred_element_type=F32) for part in _split3(x))


def _dot_split(a, b01):
    hi = a.astype(BF16)
    lo = (a - hi.astype(F32)).astype(BF16)
    return (jnp.dot(hi, b01, preferred_element_type=F32)
            + jnp.dot(lo, b01, preferred_element_type=F32))


def _sigmoid(x):
    return 1.0 / (1.0 + jnp.exp(-x))


def _softplus(x):
    return jnp.maximum(x, 0.0) + jnp.log(1.0 + jnp.exp(-jnp.abs(x)))


def _rms(x, w):
    return x * lax.rsqrt(jnp.mean(x * x, axis=-1, keepdims=True) + NORM_EPS) * w


def _shift_rows(carry8, p, j):
    ext = jnp.concatenate([carry8, p], axis=0)
    return pltpu.roll(ext, j, 0)[8:]


def _params(*sem):
    return pltpu.CompilerParams(dimension_semantics=sem, vmem_limit_bytes=VMEM_LIMIT)


_SHARED = ("hsum", "exp_ch", "exp_ln")
_FROM_SECOND_LAYER = ("v0", "v2p")


def _layer_specs(prm, names, l):
    specs, args = [], []
    for nm in names:
        arr = prm[nm]
        idx = 0 if nm in _SHARED else (l - 1 if nm in _FROM_SECOND_LAYER else l)
        nd = arr.ndim
        specs.append(pl.BlockSpec((None,) + arr.shape[1:],
                                  lambda *_, idx=idx, nd=nd: (idx,) + (0,) * (nd - 1),
                                  pipeline_mode=pl.Buffered(1)))
        args.append(arr)
    return specs, args


def _in_proj_kernel(x_ref, g_ref, w_ref, o_ref, *, chunk):
    xn = _rms(x_ref[...], g_ref[...]).astype(BF16)
    for c0 in range(0, IN_COLS_PAD, chunk):
        o_ref[:, c0:c0 + chunk] = jnp.dot(
            xn, w_ref[:, c0:c0 + chunk], preferred_element_type=F32).astype(BF16)


def in_proj(x, prm, l, tm=512, chunk=512):
    n = x.shape[0]
    specs, args = _layer_specs(prm, ["norm_mix", "w_cat"], l)
    return pl.pallas_call(
        functools.partial(_in_proj_kernel, chunk=chunk),
        out_shape=jax.ShapeDtypeStruct((n, IN_COLS_PAD), BF16),
        grid=(n // tm,),
        in_specs=[pl.BlockSpec((tm, D_MODEL), lambda i: (i, 0))] + specs,
        out_specs=pl.BlockSpec((tm, IN_COLS_PAD), lambda i: (i, 0)),
        compiler_params=_params("parallel"),
        name="in_proj",
    )(x, *args)


def _split_bf16(x):
    hi = x.astype(BF16)
    return hi, (x - hi.astype(F32)).astype(BF16)


def _pair_tiles(xb, bd):
    return [jnp.concatenate([xb[:, i:i + LANES]] * 2, axis=0) * bd
            for i in range(0, xb.shape[1], LANES)]


def _mm_tiles(lhs, tiles):
    stacked = lhs[0] if len(lhs) == 1 else jnp.concatenate(lhs, axis=0)
    return jnp.concatenate(
        [jnp.dot(stacked[:, p * LANES:(p + 1) * LANES], t, preferred_element_type=F32)
         for p, t in enumerate(tiles)], axis=1)


def _mm_split(a_hl, b_tiles_hl, m):
    r1 = _mm_tiles(list(a_hl), b_tiles_hl[0])
    return r1[:m] + r1[m:] + _mm_tiles([a_hl[0]], b_tiles_hl[1])


def _headwise_mm(a, b, bd):
    return _mm_tiles([a.astype(BF16)], _pair_tiles(b.astype(BF16), bd))


def _tri_inverse(n_cat, bd):
    size, width = n_cat.shape
    row = lax.broadcasted_iota(jnp.int32, (size, LANES), 0)
    col = lax.broadcasted_iota(jnp.int32, (size, LANES), 1) % size
    wide = lambda tile: jnp.concatenate([tile] * (width // LANES), axis=1)

    def same_block(b):
        return (row // b) == (col // b)

    a0 = jnp.where(wide(same_block(8)), n_cat, 0.0).astype(BF16)
    d0 = wide(jnp.where(row == col, 1.0, 0.0).astype(BF16)) + a0
    p = _mm_tiles([a0], _pair_tiles(a0, bd))
    p_hl = _split_bf16(p)
    p_tiles = (_pair_tiles(p_hl[0], bd), _pair_tiles(p_hl[1], bd))
    r1 = _mm_tiles([d0, p_hl[0], p_hl[1]], p_tiles[0])
    r2 = _mm_tiles([d0, p_hl[0]], p_tiles[1])
    d = d0.astype(F32) + r1[:size] + r2[:size]
    p = r1[size:2 * size] + r1[2 * size:] + r2[size:]
    p_hl = _split_bf16(p)
    d = d + _mm_split(_split_bf16(d), (_pair_tiles(p_hl[0], bd), _pair_tiles(p_hl[1], bd)), size)
    b = 8
    while b < size:
        e = jnp.where(wide(same_block(2 * b) & jnp.logical_not(same_block(b))), n_cat, 0.0).astype(BF16)
        d_hl = _split_bf16(d)
        r = _mm_tiles(list(d_hl), _pair_tiles(e, bd))
        x_hl = _split_bf16(r[:size] + r[size:])
        d = d + _mm_split(x_hl, (_pair_tiles(d_hl[0], bd), _pair_tiles(d_hl[1], bd)), size)
        b *= 2
    return d


def _rwkv_kernel(*refs, has_vres, tb, group):
    if has_vres:
        (rkv_ref, lglv_ref, lwla_ref, vfirst_ref, mu_rkv, mu_lglv, mu_lwla, w0, w2p, a0, a2p,
         g2p, kkw, kaw, rkw, gnw, gnb, hsum, v0, v2p,
         y_ref,
         c_rkv, c_lglv, c_lwla, s_ref, phi_s, psi_s, etot_s,
         r_s, k_s, v_s, kk_s, kka_s, lw_s, g_s, y_s, rhat_s, y0_s) = refs
    else:
        (rkv_ref, lglv_ref, lwla_ref, mu_rkv, mu_lglv, mu_lwla, w0, w2p, a0, a2p,
         g2p, kkw, kaw, rkw, gnw, gnb, hsum,
         y_ref, vfirst_out,
         c_rkv, c_lglv, c_lwla, s_ref, phi_s, psi_s, etot_s,
         r_s, k_s, v_s, kk_s, kka_s, lw_s, g_s, y_s, rhat_s, y0_s) = refs

    @pl.when(pl.program_id(1) == 0)
    def _():
        c_rkv[...] = jnp.zeros_like(c_rkv)
        c_lglv[...] = jnp.zeros_like(c_lglv)
        c_lwla[...] = jnp.zeros_like(c_lwla)
        s_ref[...] = jnp.zeros_like(s_ref)

    def mix(p_ref, carry, mu):
        p = p_ref[...].astype(F32)
        prev = _shift_rows(carry[...], p, 1)
        carry[...] = p[tb - 8:]
        return p + (prev - p) * mu[...]

    rkv = mix(rkv_ref, c_rkv, mu_rkv)
    lglv = mix(lglv_ref, c_lglv, mu_lglv)
    lwla = mix(lwla_ref, c_lwla, mu_lwla)
    r = rkv[:, :WIDTH]
    k = rkv[:, WIDTH:2 * WIDTH]
    v = rkv[:, 2 * WIDTH:]

    lw_s[...] = -_sigmoid(w0[...] + _dot(jnp.tanh(lwla), w2p[...])) * RW_DECAY_SCALE
    a = _sigmoid(a0[...] + _dot(lwla, a2p[...]))
    g_s[...] = _dot(_sigmoid(lglv), g2p[...])
    if has_vres:
        v = v + (vfirst_ref[...].astype(F32) - v) * _sigmoid(v0[...] + _dot(lglv, v2p[...]))
    else:
        vfirst_out[...] = v.astype(BF16)
    hs = hsum[...]
    kk = k * kkw[...]
    kk = kk * lax.rsqrt(jnp.maximum(_dot(kk * kk, hs), 1e-24))
    k = k * (1.0 + (a - 1.0) * kaw[...])
    r_s[...] = r
    k_s[...] = k
    v_s[...] = v
    kk_s[...] = kk
    kka_s[...] = kk * a

    cl = RW_CHUNK
    n_chunks = tb // cl
    row = lax.broadcasted_iota(jnp.int32, (cl, cl), 0)
    col = lax.broadcasted_iota(jnp.int32, (cl, cl), 1)
    tri_incl = jnp.where(row >= col, 1.0, 0.0).astype(BF16)
    row2 = lax.broadcasted_iota(jnp.int32, (2 * cl, 2 * LANES), 0)
    col2 = lax.broadcasted_iota(jnp.int32, (2 * cl, 2 * LANES), 1)
    m_mask = (row2 % cl - col2 % cl) >= jnp.where(row2 < cl, 1, 0)
    rr = lax.broadcasted_iota(jnp.int32, (LANES, LANES), 0)
    cc = lax.broadcasted_iota(jnp.int32, (LANES, LANES), 1)
    same_head = (rr // HEAD_DIM) == (cc // HEAD_DIM)
    lane_lo = cc < HEAD_DIM
    bd = jnp.where(same_head, 1.0, 0.0).astype(BF16)
    same_head_w = jnp.concatenate([same_head, same_head], axis=1)
    same_head_t = jnp.concatenate([same_head, same_head], axis=0)
    zeros_half = jnp.zeros((cl, LANES), F32)

    def group_pre(gi, carry):
        chunks = [gi * group + g for g in range(group)]
        rows = [pl.ds(pl.multiple_of(ci * cl, cl), cl) for ci in chunks]
        lws = [lw_s[r, :] for r in rows]
        cums = [_dot01_left(tri_incl, lw) for lw in lws]
        rts, ats, bts, kts, e_mids, e_ends = [], [], [], [], [], []
        for g in range(group):
            cum, lw = cums[g], lws[g]
            mid = cum[cl // 2 - 1:cl // 2, :]
            cen = cum - mid
            e_pos = jnp.exp(cen)
            e_neg = jnp.exp(-cen)
            rts.append(r_s[rows[g], :] * e_pos)
            ats.append(-kk_s[rows[g], :] * jnp.exp(cen - lw))
            bts.append(kka_s[rows[g], :] * e_neg)
            kts.append(k_s[rows[g], :] * e_neg)
            e_mids.append(jnp.exp(mid))
            e_ends.append(jnp.exp(cen[cl - 1:cl, :]))
            etot_s[chunks[g]] = jnp.broadcast_to(jnp.exp(cum[cl - 1:cl, :]), (8, WIDTH))
        pairs = [(g, p) for g in range(group) for p in range(N_PAIRS)]
        lane = lambda p: slice(p * LANES, (p + 1) * LANES)
        zs = [jnp.concatenate([bts[g][:, lane(p)], kts[g][:, lane(p)]], axis=0)
              for g, p in pairs]
        zts = [z.T for z in zs]
        ms = []
        for (g, p), zt in zip(pairs, zts):
            zr = pltpu.roll(zt, HEAD_DIM, 1)
            wgt = jnp.concatenate([jnp.where(lane_lo, zt, zr), jnp.where(lane_lo, zr, zt)], axis=1)
            wgt = jnp.where(same_head_w, wgt, 0.0)
            m = _dot(jnp.concatenate([ats[g][:, lane(p)], rts[g][:, lane(p)]], axis=0), wgt)
            ms.append(jnp.where(m_mask, m, 0.0))
        rt = jnp.concatenate(rts, axis=1)
        at = jnp.concatenate(ats, axis=1)
        vv = jnp.concatenate([v_s[r, :] for r in rows], axis=1)
        e_mid = jnp.concatenate(e_mids, axis=1)
        a_ab = jnp.concatenate([m[:cl, :LANES] for m in ms], axis=1)
        a_kr = jnp.concatenate(
            [jnp.concatenate([m[:cl, LANES:] for m in ms], axis=1),
             jnp.concatenate([m[cl:, LANES:] for m in ms], axis=1)], axis=0)
        a_rb = jnp.concatenate([m[cl:, :LANES] for m in ms], axis=1).astype(BF16)
        t_hl = _split_bf16(_tri_inverse(a_ab, bd))
        akv = _headwise_mm(a_kr, vv, bd)
        at_hl = _split_bf16(at)
        av_hl = _split_bf16(akv[:cl])
        at_t = _mm_split(t_hl, (_pair_tiles(at_hl[0], bd), _pair_tiles(at_hl[1], bd)), cl)
        u0 = _mm_split(t_hl, (_pair_tiles(av_hl[0], bd), _pair_tiles(av_hl[1], bd)), cl)
        rhat = (rt + _mm_tiles([a_rb], _pair_tiles(at_t.astype(BF16), bd))) * e_mid
        y0 = _mm_tiles([a_rb], _pair_tiles(u0.astype(BF16), bd)) + akv[cl:]
        at_m = at_t * e_mid
        for g in range(group):
            rhat_s[rows[g], :] = rhat[:, g * WIDTH:(g + 1) * WIDTH]
            y0_s[rows[g], :] = y0[:, g * WIDTH:(g + 1) * WIDTH]
        for i, (g, p) in enumerate(pairs):
            ln = slice(g * WIDTH + p * LANES, g * WIDTH + (p + 1) * LANES)
            lhs = jnp.concatenate(
                [jnp.concatenate([at_m[:, ln], zeros_half], axis=0),
                 jnp.concatenate([u0[:, ln], vv[:, ln]], axis=0)], axis=1)
            pp = _dot(lhs.T, zs[i]) * e_ends[g][:, lane(p)]
            pp = jnp.where(same_head_t, pp, 0.0)
            phi_s[chunks[g], p] = pp[:LANES]
            psi_s[chunks[g], p] = pp[LANES:]
        return carry

    if n_chunks == group:
        group_pre(0, 0)
    else:
        lax.fori_loop(0, n_chunks // group, group_pre, 0)

    states = [s_ref[p] for p in range(N_PAIRS)]
    for ci in range(n_chunks):
        rows = slice(ci * cl, (ci + 1) * cl)
        et = etot_s[ci][0:1]
        for p in range(N_PAIRS):
            ln = slice(p * LANES, (p + 1) * LANES)
            s0 = states[p]
            y_s[rows, ln] = _dot_nt(rhat_s[rows, ln], s0) + y0_s[rows, ln]
            states[p] = s0 * et[:, ln] + _dot(s0, phi_s[ci, p]) + psi_s[ci, p]
    for p in range(N_PAIRS):
        s_ref[p] = states[p]

    y = y_s[...]
    inv_n = 1.0 / HEAD_DIM
    mean = _dot(y, hs) * inv_n
    yc = y - mean
    var = _dot(yc * yc, hs) * inv_n
    yn = yc * lax.rsqrt(var + RW_GN_EPS) * gnw[...] + gnb[...]
    bonus = _dot(r_s[...] * k_s[...] * rkw[...], hs) * v_s[...]
    y_ref[...] = ((yn + bonus) * g_s[...]).astype(BF16)


def rwkv_mixer(proj, vfirst, prm, l, batch, seq, tb=512, group=8):
    n = proj.shape[0]
    nt = seq // tb
    has_vres = vfirst is not None
    row = lambda b, t: b * nt + t
    in_specs = [pl.BlockSpec((tb, 3 * WIDTH), lambda b, t: (row(b, t), C_RKV // (3 * WIDTH))),
                pl.BlockSpec((tb, 256), lambda b, t: (row(b, t), C_LGLV // 256)),
                pl.BlockSpec((tb, 128), lambda b, t: (row(b, t), C_LWLA // 128))]
    args = [proj, proj, proj]
    if has_vres:
        in_specs.append(pl.BlockSpec((tb, WIDTH), lambda b, t: (row(b, t), 0)))
        args.append(vfirst)
    names = ["mu_rkv", "mu_lglv", "mu_lwla", "w0", "w2p", "a0", "a2p", "g2p", "kkw", "kaw", "rkw",
             "gnw", "gnb", "hsum"]
    if has_vres:
        names += ["v0", "v2p"]
    specs, pargs = _layer_specs(prm, names, l)
    in_specs += specs
    args += pargs
    y_spec = pl.BlockSpec((tb, WIDTH), lambda b, t: (row(b, t), 0))
    y_shape = jax.ShapeDtypeStruct((n, WIDTH), BF16)
    if has_vres:
        out_shape, out_specs = y_shape, y_spec
    else:
        out_shape, out_specs = (y_shape, y_shape), (y_spec, y_spec)
    big = pltpu.VMEM((tb, WIDTH), F32)
    nc = tb // RW_CHUNK
    pair_mats = pltpu.VMEM((nc, N_PAIRS, LANES, LANES), F32)
    scratch = [pltpu.VMEM((8, 3 * WIDTH), F32), pltpu.VMEM((8, 256), F32), pltpu.VMEM((8, 128), F32),
               pltpu.VMEM((N_PAIRS, LANES, LANES), F32), pair_mats, pair_mats,
               pltpu.VMEM((nc, 8, WIDTH), F32)] + [big] * 10
    return pl.pallas_call(
        functools.partial(_rwkv_kernel, has_vres=has_vres, tb=tb, group=group),
        out_shape=out_shape, grid=(batch, nt), in_specs=in_specs, out_specs=out_specs,
        scratch_shapes=scratch, compiler_params=_params("parallel", "arbitrary"),
        name="rwkv7_vres" if has_vres else "rwkv7",
    )(*args)


def _ssd_kernel(zx_ref, dt_ref, convw, convb, dtb, a_neg, dskip, normw, exp_ch, exp_ln,
                y_ref, c_conv, st_ref, *, nb):
    q = SSM_CHUNK
    rows_all = nb * q

    @pl.when(pl.program_id(1) == 0)
    def _():
        c_conv[...] = jnp.zeros_like(c_conv)
        st_ref[...] = jnp.zeros_like(st_ref)

    zx = zx_ref[...].astype(F32)
    z_all = zx[:, :WIDTH]
    xbc = zx[:, WIDTH:]
    carry = c_conv[...]
    conv = xbc * convw[SSM_CONV - 1:SSM_CONV, :] + convb[...]
    for j in range(1, SSM_CONV):
        conv = conv + _shift_rows(carry, xbc, j) * convw[SSM_CONV - 1 - j:SSM_CONV - j, :]
    c_conv[...] = xbc[rows_all - 8:]
    xbc = conv * _sigmoid(conv)
    dt_all = _softplus(dt_ref[...].astype(F32) + dtb[...])
    a_all = dt_all * a_neg[...]
    row = lax.broadcasted_iota(jnp.int32, (q, q), 0)
    col = lax.broadcasted_iota(jnp.int32, (q, q), 1)
    causal = row >= col
    tri = jnp.where(causal, 1.0, 0.0).astype(BF16)
    ech = exp_ch[...]
    eln = exp_ln[...]
    lo = lax.broadcasted_iota(jnp.int32, (q, LANES), 1) < HEAD_DIM
    per_g = N_HEADS // SSM_GROUPS
    gw = per_g * HEAD_DIM
    states = [st_ref[g] for g in range(SSM_GROUPS)]
    for sb in range(nb):
        rs = slice(sb * q, (sb + 1) * q)
        xs = xbc[rs, :WIDTH]
        bm = xbc[rs, WIDTH:WIDTH + SSM_GROUPS * SSM_STATE]
        cm = xbc[rs, WIDTH + SSM_GROUPS * SSM_STATE:]
        dt = dt_all[rs]
        acum = _dot01_left(tri, a_all[rs])
        acum_t = acum.T
        dt_x = _dot_split(dt, ech)
        acum_x = _dot01_right(acum, ech)
        atot_x = acum_x[q - 1:q, :]
        acum_l = _dot01_right(acum, eln)
        xdt = xs * dt_x
        xdec = xdt * jnp.exp(atot_x - acum_x)
        e_in = jnp.exp(acum_x)
        e_tot = jnp.exp(atot_x)
        ys = []
        for g in range(SSM_GROUPS):
            bg = bm[:, g * SSM_STATE:(g + 1) * SSM_STATE]
            cg = cm[:, g * SSM_STATE:(g + 1) * SSM_STATE]
            cb = _dot_nt(cg, bg)
            gl = slice(g * gw, (g + 1) * gw)
            y_g = _dot(cg, states[g]) * e_in[:, gl]
            states[g] = states[g] * e_tot[:, gl] + _dot(bg.T, xdec[:, gl])
            diag = []
            for pp in range(per_g // 2):
                xp = xdt[:, gl][:, pp * LANES:(pp + 1) * LANES]
                acc = None
                for hh in range(2):
                    h = g * per_g + 2 * pp + hh
                    diff = acum_l[:, h * LANES:(h + 1) * LANES] - acum_t[h:h + 1, :]
                    lmat = jnp.where(causal, jnp.exp(jnp.where(causal, diff, 0.0)), 0.0)
                    xh = jnp.where(lo, xp, 0.0) if hh == 0 else jnp.where(lo, 0.0, xp)
                    t = _dot(cb * lmat, xh)
                    acc = t if acc is None else acc + t
                diag.append(acc)
            ys.append(y_g + jnp.concatenate(diag, axis=1))
        y = jnp.concatenate(ys, axis=1) + xs * dskip[...]
        z = z_all[rs]
        yz = y * (z * _sigmoid(z))
        outs = []
        for g in range(SSM_GROUPS):
            yg = yz[:, g * gw:(g + 1) * gw]
            outs.append(yg * lax.rsqrt(jnp.mean(yg * yg, axis=-1, keepdims=True) + NORM_EPS))
        y_ref[rs, :] = (jnp.concatenate(outs, axis=1) * normw[...]).astype(BF16)
    for g in range(SSM_GROUPS):
        st_ref[g] = states[g]


def ssd_mixer(proj, prm, l, batch, seq, nb=4):
    n = proj.shape[0]
    rows = nb * SSM_CHUNK
    nt = seq // rows
    row = lambda b, t: b * nt + t
    names = ["convw", "convb", "dtb", "a_neg", "dskip", "normw", "exp_ch", "exp_ln"]
    specs, args = _layer_specs(prm, names, l)
    in_specs = [pl.BlockSpec((rows, 3 * WIDTH), lambda b, t: (row(b, t), C_SSM // (3 * WIDTH))),
                pl.BlockSpec((rows, 128), lambda b, t: (row(b, t), C_DT // 128))] + specs
    return pl.pallas_call(
        functools.partial(_ssd_kernel, nb=nb),
        out_shape=jax.ShapeDtypeStruct((n, WIDTH), BF16),
        grid=(batch, nt), in_specs=in_specs,
        out_specs=pl.BlockSpec((rows, WIDTH), lambda b, t: (row(b, t), 0)),
        scratch_shapes=[pltpu.VMEM((8, WIDTH + 2 * SSM_GROUPS * SSM_STATE), F32),
                        pltpu.VMEM((SSM_GROUPS, SSM_STATE, WIDTH // SSM_GROUPS), F32)],
        compiler_params=_params("parallel", "arbitrary"),
        name="ssd",
    )(proj, proj, *args)


def _swa_kernel(sink_ref, qkv_ref, y_ref, kv_prev, *, nb, layer):
    wdw = WINDOW

    @pl.when(pl.program_id(1) == 0)
    def _():
        kv_prev[...] = jnp.zeros_like(kv_prev)

    qkv = qkv_ref[...].astype(F32)
    qi = lax.broadcasted_iota(jnp.int32, (wdw, 2 * wdw), 0)
    kj = lax.broadcasted_iota(jnp.int32, (wdw, 2 * wdw), 1)
    rel = qi + wdw - kj
    in_window = (rel >= 0) & (rel < wdw)
    first = jnp.where(pl.program_id(1) > 0, 0, wdw)
    lo = lax.broadcasted_iota(jnp.int32, (wdw, LANES), 1) < HEAD_DIM
    lo2 = lax.broadcasted_iota(jnp.int32, (2 * wdw, LANES), 1) < HEAD_DIM
    rep = N_HEADS // ATT_KV_HEADS
    scale = HEAD_DIM ** -0.5
    bands, scores = [], []
    prev = kv_prev[...]
    for sb in range(nb):
        rs = slice(sb * wdw, (sb + 1) * wdw)
        kv = qkv[rs, WIDTH:]
        band = jnp.concatenate([prev, kv], axis=0)
        prev = kv
        bands.append(band)
        valid = in_window & (kj >= first) if sb == 0 else in_window
        for h in range(N_HEADS):
            g = h // rep
            qp = qkv[rs, (h // 2) * LANES:(h // 2 + 1) * LANES]
            qh = jnp.where(lo, qp, 0.0) if h % 2 == 0 else jnp.where(lo, 0.0, qp)
            s = _dot_nt(qh, band[:, g * LANES:(g + 1) * LANES]) * scale
            scores.append(jnp.where(valid, s, -1e30))
    kv_prev[...] = prev
    probs = []
    for i, s in enumerate(scores):
        sink = sink_ref[layer, i % N_HEADS]
        m = jnp.maximum(jnp.max(s, axis=-1, keepdims=True), sink)
        p = jnp.exp(s - m)
        den = jnp.sum(p, axis=-1, keepdims=True) + jnp.exp(sink - m)
        probs.append(p / den)
    for sb in range(nb):
        outs = []
        for tile in range(N_HEADS // 2):
            g = (2 * tile) // rep
            vb = bands[sb][:, (ATT_KV_HEADS + g) * LANES:(ATT_KV_HEADS + g + 1) * LANES]
            outs.append(_dot(probs[sb * N_HEADS + 2 * tile], jnp.where(lo2, vb, 0.0))
                        + _dot(probs[sb * N_HEADS + 2 * tile + 1], jnp.where(lo2, 0.0, vb)))
        y_ref[sb * wdw:(sb + 1) * wdw, :] = jnp.concatenate(outs, axis=1).astype(BF16)


def swa_mixer(proj, sinks, l, batch, seq, nb=4):
    n = proj.shape[0]
    rows = nb * WINDOW
    nt = seq // rows
    row = lambda b, t: b * nt + t
    return pl.pallas_call(
        functools.partial(_swa_kernel, nb=nb, layer=l),
        out_shape=jax.ShapeDtypeStruct((n, WIDTH), BF16),
        grid=(batch, nt),
        in_specs=[pl.BlockSpec(memory_space=pltpu.SMEM),
                  pl.BlockSpec((rows, 2 * WIDTH), lambda b, t: (row(b, t), C_ATT // (2 * WIDTH)))],
        out_specs=pl.BlockSpec((rows, WIDTH), lambda b, t: (row(b, t), 0)),
        scratch_shapes=[pltpu.VMEM((WINDOW, WIDTH), F32)],
        compiler_params=_params("parallel", "arbitrary"),
        name="swa",
    )(sinks, proj)


def _merge_ffn_kernel(x_ref, gate_ref, yrw, yssm, yatt, gb, wrw, wssm, watt, wout, gamma, wgu, wd,
                      gfin, o_ref, act, *, chunk, final_norm):
    merged = None
    for i, (y, w) in enumerate(((yrw, wrw), (yssm, wssm), (yatt, watt))):
        cols = slice(i * D_MODEL, (i + 1) * D_MODEL)
        gate = _sigmoid(gate_ref[:, cols].astype(F32) + gb[:, cols])
        t = gate * jnp.dot(y[...], w[...], preferred_element_type=F32)
        merged = t if merged is None else merged + t
    x = x_ref[...] + _dot(merged, wout[...])
    xn = _rms(x, gamma[...]).astype(BF16)
    for c0 in range(0, FFN_HIDDEN, chunk):
        gate = jnp.dot(xn, wgu[:, c0:c0 + chunk], preferred_element_type=F32)
        up = jnp.dot(xn, wgu[:, FFN_HIDDEN + c0:FFN_HIDDEN + c0 + chunk], preferred_element_type=F32)
        act[:, c0:c0 + chunk] = (gate * _sigmoid(gate) * up).astype(BF16)
    out = x + jnp.dot(act[...], wd[...], preferred_element_type=F32)
    if final_norm:
        out = _rms(out, gfin[...])
    o_ref[...] = out


def merge_ffn(x, proj, y_rw, y_ssm, y_att, prm, l, final_norm, tm=512, chunk=256):
    n = x.shape[0]
    names = ["gate_b", "w_br_rw", "w_br_ssm", "w_br_att", "w_out", "norm_ffn", "w_gu", "w_down"]
    specs, args = _layer_specs(prm, names, l)
    fspecs, fargs = _layer_specs(prm, ["norm_final"], 0)
    rows = lambda width: pl.BlockSpec((tm, width), lambda i: (i, 0))
    return pl.pallas_call(
        functools.partial(_merge_ffn_kernel, chunk=chunk, final_norm=final_norm),
        out_shape=jax.ShapeDtypeStruct((n, D_MODEL), F32),
        grid=(n // tm,),
        in_specs=[rows(D_MODEL), rows(3 * D_MODEL), rows(WIDTH), rows(WIDTH), rows(WIDTH)]
        + specs + fspecs,
        out_specs=rows(D_MODEL),
        scratch_shapes=[pltpu.VMEM((tm, FFN_HIDDEN), BF16)],
        compiler_params=_params("parallel"),
        name="merge_ffn_final" if final_norm else "merge_ffn",
    )(x, proj, y_rw, y_ssm, y_att, *args, *fargs)


def _pad_axis(w, size, axis, at=0):
    pads = [(0, 0)] * w.ndim
    pads[axis] = (at, size - at - w.shape[axis])
    return jnp.pad(w, pads)


def _prep_params(p):
    depth = p["w_in"].shape[0]
    rw_cols = 3 * WIDTH + RW_DECAY_LORA + RW_ICLR_LORA + RW_GATE_LORA
    ssm_cols = WIDTH + (WIDTH + 2 * SSM_GROUPS * SSM_STATE) + N_HEADS
    att_cols = WIDTH + 2 * ATT_KV_HEADS * HEAD_DIM
    o_ssm = rw_cols
    o_att = o_ssm + ssm_cols
    o_gate = o_att + att_cols
    o_k = o_att + WIDTH
    o_v = o_k + ATT_KV_HEADS * HEAD_DIM
    w = p["w_in"].astype(BF16)
    zeros = lambda c: jnp.zeros((depth, D_MODEL, c), w.dtype)
    dup = lambda o: [w[:, :, o + HEAD_DIM * (i // 2):o + HEAD_DIM * (i // 2 + 1)] for i in range(4)]
    lv = jnp.concatenate([zeros(RW_VRES_LORA)[:1], p["rw_vres_down"].astype(BF16)], axis=0)
    pad_lglv = 256 - RW_GATE_LORA - RW_VRES_LORA
    w_cat = jnp.concatenate(
        [w[:, :, o_gate:o_gate + 3 * D_MODEL],
         w[:, :, :3 * WIDTH],
         w[:, :, o_ssm:o_ssm + 3 * WIDTH],
         w[:, :, o_att:o_att + WIDTH]] + dup(o_k) + dup(o_v)
        + [w[:, :, 3 * WIDTH + 128:rw_cols], lv, zeros(pad_lglv),
           w[:, :, 3 * WIDTH:3 * WIDTH + 128],
           w[:, :, o_ssm + 3 * WIDTH:o_ssm + 3 * WIDTH + N_HEADS], zeros(128 - N_HEADS)], axis=2)
    assert w_cat.shape == (depth, D_MODEL, IN_COLS_PAD)

    row = lambda v: v.reshape(v.shape[0], 1, -1).astype(F32)
    mu = p["rw_mu"]
    vmu = jnp.concatenate([jnp.zeros((1, RW_VRES_LORA), F32), p["rw_vres_mu"]], axis=0)
    head_of = jnp.arange(WIDTH) // HEAD_DIM
    return {
        "norm_mix": row(p["norm_mix"]),
        "w_cat": w_cat,
        "mu_rkv": row(mu[:, :3 * WIDTH]),
        "mu_lwla": row(mu[:, 3 * WIDTH:3 * WIDTH + 128]),
        "mu_lglv": row(jnp.concatenate([mu[:, 3 * WIDTH + 128:], vmu,
                                        jnp.zeros((depth, pad_lglv), F32)], axis=1)),
        "w0": row(p["rw_w0"]),
        "w2p": _pad_axis(p["rw_w2"], 128, 1, 0).astype(BF16),
        "a0": row(p["rw_a0"]),
        "a2p": _pad_axis(p["rw_a2"], 128, 1, RW_DECAY_LORA).astype(BF16),
        "g2p": _pad_axis(p["rw_g2"], 256, 1, 0).astype(BF16),
        "kkw": row(p["rw_k_k"]), "kaw": row(p["rw_k_a"]), "rkw": row(p["rw_r_k"]),
        "gnw": row(p["rw_gn_w"]), "gnb": row(p["rw_gn_b"]),
        "v0": row(p["rw_vres_v0"]),
        "v2p": _pad_axis(p["rw_vres_v2"], 256, 1, RW_GATE_LORA).astype(BF16),
        "hsum": (head_of[:, None] == head_of[None, :]).astype(BF16)[None],
        "convw": p["ssm_conv_w"].astype(F32),
        "convb": row(p["ssm_conv_b"]),
        "dtb": row(_pad_axis(p["ssm_dt_bias"], 128, 1)),
        "a_neg": row(_pad_axis(-jnp.exp(p["ssm_a_log"].astype(F32)), 128, 1)),
        "dskip": row(jnp.repeat(p["ssm_d"], HEAD_DIM, axis=1)),
        "normw": row(p["ssm_norm_w"]),
        "exp_ch": (jnp.arange(128)[:, None] == head_of[None, :]).astype(BF16)[None],
        "exp_ln": (jnp.arange(128)[:, None]
                   == (jnp.arange(N_HEADS * LANES) // LANES)[None, :]).astype(BF16)[None],
        "sinks": p["att_sinks"].astype(F32),
        "gate_b": row(p["gate_b"]),
        "w_br_rw": p["w_br_rw"].astype(BF16),
        "w_br_ssm": p["w_br_ssm"].astype(BF16),
        "w_br_att": p["w_br_att"].astype(BF16),
        "w_out": p["w_out"].astype(BF16),
        "norm_ffn": row(p["norm_ffn"]),
        "w_gu": p["ffn_w_gu"].astype(BF16),
        "w_down": p["ffn_w_down"].astype(BF16),
        "norm_final": p["norm_final"].reshape(1, 1, -1).astype(F32),
    }


def kernel(x, norm_mix, w_in, rw_mu, rw_w0, rw_w2, rw_a0, rw_a2, rw_g2, rw_k_k, rw_k_a, rw_r_k, rw_gn_w, rw_gn_b, rw_vres_down, rw_vres_mu, rw_vres_v0, rw_vres_v2, ssm_conv_w, ssm_conv_b, ssm_dt_bias, ssm_a_log, ssm_d, ssm_norm_w, att_sinks, gate_b, w_br_rw, w_br_ssm, w_br_att, w_out, norm_ffn, ffn_w_gu, ffn_w_down, norm_final):
    p = dict(norm_mix=norm_mix, w_in=w_in, rw_mu=rw_mu, rw_w0=rw_w0, rw_w2=rw_w2, rw_a0=rw_a0,
             rw_a2=rw_a2, rw_g2=rw_g2, rw_k_k=rw_k_k, rw_k_a=rw_k_a,
             rw_r_k=rw_r_k.reshape(rw_r_k.shape[0], -1), rw_gn_w=rw_gn_w, rw_gn_b=rw_gn_b,
             rw_vres_down=rw_vres_down, rw_vres_mu=rw_vres_mu, rw_vres_v0=rw_vres_v0,
             rw_vres_v2=rw_vres_v2, ssm_conv_w=ssm_conv_w, ssm_conv_b=ssm_conv_b,
             ssm_dt_bias=ssm_dt_bias, ssm_a_log=ssm_a_log, ssm_d=ssm_d, ssm_norm_w=ssm_norm_w,
             att_sinks=att_sinks, gate_b=gate_b, w_br_rw=w_br_rw, w_br_ssm=w_br_ssm,
             w_br_att=w_br_att, w_out=w_out, norm_ffn=norm_ffn, ffn_w_gu=ffn_w_gu,
             ffn_w_down=ffn_w_down, norm_final=norm_final)
    batch, seq, _ = x.shape
    depth = w_in.shape[0]
    prm = _prep_params(p)
    xf = x.reshape(batch * seq, D_MODEL)
    vfirst = None
    for l in range(depth):
        proj = in_proj(xf, prm, l)
        if l == 0:
            y_rw, vfirst = rwkv_mixer(proj, None, prm, l, batch, seq)
        else:
            y_rw = rwkv_mixer(proj, vfirst, prm, l, batch, seq)
        y_ssm = ssd_mixer(proj, prm, l, batch, seq)
        y_att = swa_mixer(proj, prm["sinks"], l, batch, seq)
        xf = merge_ffn(xf, proj, y_rw, y_ssm, y_att, prm, l, final_norm=(l == depth - 1))
    return xf.reshape(batch, seq, D_MODEL)
```

```python
import functools

import jax
import jax.numpy as jnp
from jax import lax
from jax.experimental import pallas as pl
from jax.experimental.pallas import tpu as pltpu

F32 = jnp.float32
BF16 = jnp.bfloat16

D_MODEL = 1024
HEAD_DIM = 64
N_HEADS = 8
WIDTH = N_HEADS * HEAD_DIM
LANES = 128
N_PAIRS = WIDTH // LANES
RW_CHUNK = 64
RW_DECAY_LORA = 64
RW_ICLR_LORA = 64
RW_VRES_LORA = 32
RW_GATE_LORA = 160
RW_GN_EPS = 64e-5
RW_DECAY_SCALE = 0.6065306597126334
SSM_STATE = 128
SSM_GROUPS = 2
SSM_CHUNK = 128
SSM_CONV = 4
ATT_KV_HEADS = 2
WINDOW = 128
FFN_HIDDEN = 2816
NORM_EPS = 1e-6
VMEM_LIMIT = 56 * 1024 * 1024

C_RKV = 0
C_SSM = 1536
C_ATT = 3072
C_LGLV = 4096
C_LWLA = 4352
C_DT = 4480
IN_COLS_PAD = 4608


def _dot(a, b):
    return jnp.dot(a.astype(BF16), b.astype(BF16), preferred_element_type=F32)


def _dot_nt(a, b):
    return lax.dot_general(a.astype(BF16), b.astype(BF16), (((1,), (1,)), ((), ())),
                           preferred_element_type=F32)


def _split3(x):
    hi = x.astype(BF16)
    r1 = x - hi.astype(F32)
    mid = r1.astype(BF16)
    return hi, mid, (r1 - mid.astype(F32)).astype(BF16)


def _dot01_left(m01, x):
    return sum(jnp.dot(m01, part, preferred_element_type=F32) for part in _split3(x))


def _dot01_right(x, m01):
    return sum(jnp.dot(part, m01, preferred_element_type=F32) for part in _split3(x))


def _dot_split(a, b01):
    hi = a.astype(BF16)
    lo = (a - hi.astype(F32)).astype(BF16)
    return (jnp.dot(hi, b01, preferred_element_type=F32)
            + jnp.dot(lo, b01, preferred_element_type=F32))


def _sigmoid(x):
    return 1.0 / (1.0 + jnp.exp(-x))


def _softplus(x):
    return jnp.maximum(x, 0.0) + jnp.log(1.0 + jnp.exp(-jnp.abs(x)))


def _rms(x, w):
    return x * lax.rsqrt(jnp.mean(x * x, axis=-1, keepdims=True) + NORM_EPS) * w


def _shift_rows(carry8, p, j):
    ext = jnp.concatenate([carry8, p], axis=0)
    return pltpu.roll(ext, j, 0)[8:]


def _params(*sem):
    return pltpu.CompilerParams(dimension_semantics=sem, vmem_limit_bytes=VMEM_LIMIT)


_SHARED = ("hsum", "exp_ch", "exp_ln")
_FROM_SECOND_LAYER = ("v0", "v2p")


def _layer_specs(prm, names, l):
    specs, args = [], []
    for nm in names:
        arr = prm[nm]
        idx = 0 if nm in _SHARED else (l - 1 if nm in _FROM_SECOND_LAYER else l)
        nd = arr.ndim
        specs.append(pl.BlockSpec((None,) + arr.shape[1:],
                                  lambda *_, idx=idx, nd=nd: (idx,) + (0,) * (nd - 1),
                                  pipeline_mode=pl.Buffered(1)))
        args.append(arr)
    return specs, args


def _in_proj_kernel(x_ref, g_ref, w_ref, o_ref, *, chunk):
    xn = _rms(x_ref[...], g_ref[...]).astype(BF16)
    for c0 in range(0, IN_COLS_PAD, chunk):
        o_ref[:, c0:c0 + chunk] = jnp.dot(
            xn, w_ref[:, c0:c0 + chunk], preferred_element_type=F32).astype(BF16)


def in_proj(x, prm, l, tm=512, chunk=512):
    n = x.shape[0]
    specs, args = _layer_specs(prm, ["norm_mix", "w_cat"], l)
    return pl.pallas_call(
        functools.partial(_in_proj_kernel, chunk=chunk),
        out_shape=jax.ShapeDtypeStruct((n, IN_COLS_PAD), BF16),
        grid=(n // tm,),
        in_specs=[pl.BlockSpec((tm, D_MODEL), lambda i: (i, 0))] + specs,
        out_specs=pl.BlockSpec((tm, IN_COLS_PAD), lambda i: (i, 0)),
        compiler_params=_params("parallel"),
        name="in_proj",
    )(x, *args)


def _split_bf16(x):
    hi = x.astype(BF16)
    return hi, (x - hi.astype(F32)).astype(BF16)


def _pair_tiles(xb, bd):
    return [jnp.concatenate([xb[:, i:i + LANES]] * 2, axis=0) * bd
            for i in range(0, xb.shape[1], LANES)]


def _mm_tiles(lhs, tiles):
    stacked = lhs[0] if len(lhs) == 1 else jnp.concatenate(lhs, axis=0)
    return jnp.concatenate(
        [jnp.dot(stacked[:, p * LANES:(p + 1) * LANES], t, preferred_element_type=F32)
         for p, t in enumerate(tiles)], axis=1)


def _mm_split(a_hl, b_tiles_hl, m):
    r1 = _mm_tiles(list(a_hl), b_tiles_hl[0])
    return r1[:m] + r1[m:] + _mm_tiles([a_hl[0]], b_tiles_hl[1])


def _headwise_mm(a, b, bd):
    return _mm_tiles([a.astype(BF16)], _pair_tiles(b.astype(BF16), bd))


def _tri_inverse(n_cat, bd):
    size, width = n_cat.shape
    row = lax.broadcasted_iota(jnp.int32, (size, LANES), 0)
    col = lax.broadcasted_iota(jnp.int32, (size, LANES), 1) % size
    wide = lambda tile: jnp.concatenate([tile] * (width // LANES), axis=1)

    def same_block(b):
        return (row // b) == (col // b)

    a0 = jnp.where(wide(same_block(8)), n_cat, 0.0).astype(BF16)
    d0 = wide(jnp.where(row == col, 1.0, 0.0).astype(BF16)) + a0
    p = _mm_tiles([a0], _pair_tiles(a0, bd))
    p_hl = _split_bf16(p)
    p_tiles = (_pair_tiles(p_hl[0], bd), _pair_tiles(p_hl[1], bd))
    r1 = _mm_tiles([d0, p_hl[0], p_hl[1]], p_tiles[0])
    r2 = _mm_tiles([d0, p_hl[0]], p_tiles[1])
    d = d0.astype(F32) + r1[:size] + r2[:size]
    p = r1[size:2 * size] + r1[2 * size:] + r2[size:]
    p_hl = _split_bf16(p)
    d = d + _mm_split(_split_bf16(d), (_pair_tiles(p_hl[0], bd), _pair_tiles(p_hl[1], bd)), size)
    b = 8
    while b < size:
        e = jnp.where(wide(same_block(2 * b) & jnp.logical_not(same_block(b))), n_cat, 0.0).astype(BF16)
        d_hl = _split_bf16(d)
        r = _mm_tiles(list(d_hl), _pair_tiles(e, bd))
        x_hl = _split_bf16(r[:size] + r[size:])
        d = d + _mm_split(x_hl, (_pair_tiles(d_hl[0], bd), _pair_tiles(d_hl[1], bd)), size)
        b *= 2
    return d


def _rwkv_kernel(*refs, has_vres, tb, group):
    if has_vres:
        (rkv_ref, lglv_ref, lwla_ref, vfirst_ref, mu_rkv, mu_lglv, mu_lwla, w0, w2p, a0, a2p,
         g2p, kkw, kaw, rkw, gnw, gnb, hsum, v0, v2p,
         y_ref,
         c_rkv, c_lglv, c_lwla, s_ref, phi_s, psi_s, etot_s,
         r_s, k_s, v_s, kk_s, kka_s, lw_s, g_s, y_s, rhat_s, y0_s) = refs
    else:
        (rkv_ref, lglv_ref, lwla_ref, mu_rkv, mu_lglv, mu_lwla, w0, w2p, a0, a2p,
         g2p, kkw, kaw, rkw, gnw, gnb, hsum,
         y_ref, vfirst_out,
         c_rkv, c_lglv, c_lwla, s_ref, phi_s, psi_s, etot_s,
         r_s, k_s, v_s, kk_s, kka_s, lw_s, g_s, y_s, rhat_s, y0_s) = refs

    @pl.when(pl.program_id(1) == 0)
    def _():
        c_rkv[...] = jnp.zeros_like(c_rkv)
        c_lglv[...] = jnp.zeros_like(c_lglv)
        c_lwla[...] = jnp.zeros_like(c_lwla)
        s_ref[...] = jnp.zeros_like(s_ref)

    def mix(p_ref, carry, mu):
        p = p_ref[...].astype(F32)
        prev = _shift_rows(carry[...], p, 1)
        carry[...] = p[tb - 8:]
        return p + (prev - p) * mu[...]

    rkv = mix(rkv_ref, c_rkv, mu_rkv)
    lglv = mix(lglv_ref, c_lglv, mu_lglv)
    lwla = mix(lwla_ref, c_lwla, mu_lwla)
    r = rkv[:, :WIDTH]
    k = rkv[:, WIDTH:2 * WIDTH]
    v = rkv[:, 2 * WIDTH:]

    lw_s[...] = -_sigmoid(w0[...] + _dot(jnp.tanh(lwla), w2p[...])) * RW_DECAY_SCALE
    a = _sigmoid(a0[...] + _dot(lwla, a2p[...]))
    g_s[...] = _dot(_sigmoid(lglv), g2p[...])
    if has_vres:
        v = v + (vfirst_ref[...].astype(F32) - v) * _sigmoid(v0[...] + _dot(lglv, v2p[...]))
    else:
        vfirst_out[...] = v.astype(BF16)
    hs = hsum[...]
    kk = k * kkw[...]
    kk = kk * lax.rsqrt(jnp.maximum(_dot(kk * kk, hs), 1e-24))
    k = k * (1.0 + (a - 1.0) * kaw[...])
    r_s[...] = r
    k_s[...] = k
    v_s[...] = v
    kk_s[...] = kk
    kka_s[...] = kk * a

    cl = RW_CHUNK
    n_chunks = tb // cl
    row = lax.broadcasted_iota(jnp.int32, (cl, cl), 0)
    col = lax.broadcasted_iota(jnp.int32, (cl, cl), 1)
    tri_incl = jnp.where(row >= col, 1.0, 0.0).astype(BF16)
    row2 = lax.broadcasted_iota(jnp.int32, (2 * cl, 2 * LANES), 0)
    col2 = lax.broadcasted_iota(jnp.int32, (2 * cl, 2 * LANES), 1)
    m_mask = (row2 % cl - col2 % cl) >= jnp.where(row2 < cl, 1, 0)
    rr = lax.broadcasted_iota(jnp.int32, (LANES, LANES), 0)
    cc = lax.broadcasted_iota(jnp.int32, (LANES, LANES), 1)
    same_head = (rr // HEAD_DIM) == (cc // HEAD_DIM)
    lane_lo = cc < HEAD_DIM
    bd = jnp.where(same_head, 1.0, 0.0).astype(BF16)
    same_head_w = jnp.concatenate([same_head, same_head], axis=1)
    same_head_t = jnp.concatenate([same_head, same_head], axis=0)
    zeros_half = jnp.zeros((cl, LANES), F32)

    def group_pre(gi, carry):
        chunks = [gi * group + g for g in range(group)]
        rows = [pl.ds(pl.multiple_of(ci * cl, cl), cl) for ci in chunks]
        lws = [lw_s[r, :] for r in rows]
        cums = [_dot01_left(tri_incl, lw) for lw in lws]
        rts, ats, bts, kts, e_mids, e_ends = [], [], [], [], [], []
        for g in range(group):
            cum, lw = cums[g], lws[g]
            mid = cum[cl // 2 - 1:cl // 2, :]
            cen = cum - mid
            e_pos = jnp.exp(cen)
            e_neg = jnp.exp(-cen)
            rts.append(r_s[rows[g], :] * e_pos)
            ats.append(-kk_s[rows[g], :] * jnp.exp(cen - lw))
            bts.append(kka_s[rows[g], :] * e_neg)
            kts.append(k_s[rows[g], :] * e_neg)
            e_mids.append(jnp.exp(mid))
            e_ends.append(jnp.exp(cen[cl - 1:cl, :]))
            etot_s[chunks[g]] = jnp.broadcast_to(jnp.exp(cum[cl - 1:cl, :]), (8, WIDTH))
        pairs = [(g, p) for g in range(group) for p in range(N_PAIRS)]
        lane = lambda p: slice(p * LANES, (p + 1) * LANES)
        zs = [jnp.concatenate([bts[g][:, lane(p)], kts[g][:, lane(p)]], axis=0)
              for g, p in pairs]
        zts = [z.T for z in zs]
        ms = []
        for (g, p), zt in zip(pairs, zts):
            zr = pltpu.roll(zt, HEAD_DIM, 1)
            wgt = jnp.concatenate([jnp.where(lane_lo, zt, zr), jnp.where(lane_lo, zr, zt)], axis=1)
            wgt = jnp.where(same_head_w, wgt, 0.0)
            m = _dot(jnp.concatenate([ats[g][:, lane(p)], rts[g][:, lane(p)]], axis=0), wgt)
            ms.append(jnp.where(m_mask, m, 0.0))
        rt = jnp.concatenate(rts, axis=1)
        at = jnp.concatenate(ats, axis=1)
        vv = jnp.concatenate([v_s[r, :] for r in rows], axis=1)
        e_mid = jnp.concatenate(e_mids, axis=1)
        a_ab = jnp.concatenate([m[:cl, :LANES] for m in ms], axis=1)
        a_kr = jnp.concatenate(
            [jnp.concatenate([m[:cl, LANES:] for m in ms], axis=1),
             jnp.concatenate([m[cl:, LANES:] for m in ms], axis=1)], axis=0)
        a_rb = jnp.concatenate([m[cl:, :LANES] for m in ms], axis=1).astype(BF16)
        t_hl = _split_bf16(_tri_inverse(a_ab, bd))
        akv = _headwise_mm(a_kr, vv, bd)
        at_hl = _split_bf16(at)
        av_hl = _split_bf16(akv[:cl])
        at_t = _mm_split(t_hl, (_pair_tiles(at_hl[0], bd), _pair_tiles(at_hl[1], bd)), cl)
        u0 = _mm_split(t_hl, (_pair_tiles(av_hl[0], bd), _pair_tiles(av_hl[1], bd)), cl)
        rhat = (rt + _mm_tiles([a_rb], _pair_tiles(at_t.astype(BF16), bd))) * e_mid
        y0 = _mm_tiles([a_rb], _pair_tiles(u0.astype(BF16), bd)) + akv[cl:]
        at_m = at_t * e_mid
        for g in range(group):
            rhat_s[rows[g], :] = rhat[:, g * WIDTH:(g + 1) * WIDTH]
            y0_s[rows[g], :] = y0[:, g * WIDTH:(g + 1) * WIDTH]
        for i, (g, p) in enumerate(pairs):
            ln = slice(g * WIDTH + p * LANES, g * WIDTH + (p + 1) * LANES)
            lhs = jnp.concatenate(
                [jnp.concatenate([at_m[:, ln], zeros_half], axis=0),
                 jnp.concatenate([u0[:, ln], vv[:, ln]], axis=0)], axis=1)
            pp = _dot(lhs.T, zs[i]) * e_ends[g][:, lane(p)]
            pp = jnp.where(same_head_t, pp, 0.0)
            phi_s[chunks[g], p] = pp[:LANES]
            psi_s[chunks[g], p] = pp[LANES:]
        return carry

    if n_chunks == group:
        group_pre(0, 0)
    else:
        lax.fori_loop(0, n_chunks // group, group_pre, 0)

    states = [s_ref[p] for p in range(N_PAIRS)]
    for ci in range(n_chunks):
        rows = slice(ci * cl, (ci + 1) * cl)
        et = etot_s[ci][0:1]
        for p in range(N_PAIRS):
            ln = slice(p * LANES, (p + 1) * LANES)
            s0 = states[p]
            y_s[rows, ln] = _dot_nt(rhat_s[rows, ln], s0) + y0_s[rows, ln]
            states[p] = s0 * et[:, ln] + _dot(s0, phi_s[ci, p]) + psi_s[ci, p]
    for p in range(N_PAIRS):
        s_ref[p] = states[p]

    y = y_s[...]
    inv_n = 1.0 / HEAD_DIM
    mean = _dot(y, hs) * inv_n
    yc = y - mean
    var = _dot(yc * yc, hs) * inv_n
    yn = yc * lax.rsqrt(var + RW_GN_EPS) * gnw[...] + gnb[...]
    bonus = _dot(r_s[...] * k_s[...] * rkw[...], hs) * v_s[...]
    y_ref[...] = ((yn + bonus) * g_s[...]).astype(BF16)


def rwkv_mixer(proj, vfirst, prm, l, batch, seq, tb=512, group=8):
    n = proj.shape[0]
    nt = seq // tb
    has_vres = vfirst is not None
    row = lambda b, t: b * nt + t
    in_specs = [pl.BlockSpec((tb, 3 * WIDTH), lambda b, t: (row(b, t), C_RKV // (3 * WIDTH))),
                pl.BlockSpec((tb, 256), lambda b, t: (row(b, t), C_LGLV // 256)),
                pl.BlockSpec((tb, 128), lambda b, t: (row(b, t), C_LWLA // 128))]
    args = [proj, proj, proj]
    if has_vres:
        in_specs.append(pl.BlockSpec((tb, WIDTH), lambda b, t: (row(b, t), 0)))
        args.append(vfirst)
    names = ["mu_rkv", "mu_lglv", "mu_lwla", "w0", "w2p", "a0", "a2p", "g2p", "kkw", "kaw", "rkw",
             "gnw", "gnb", "hsum"]
    if has_vres:
        names += ["v0", "v2p"]
    specs, pargs = _layer_specs(prm, names, l)
    in_specs += specs
    args += pargs
    y_spec = pl.BlockSpec((tb, WIDTH), lambda b, t: (row(b, t), 0))
    y_shape = jax.ShapeDtypeStruct((n, WIDTH), BF16)
    if has_vres:
        out_shape, out_specs = y_shape, y_spec
    else:
        out_shape, out_specs = (y_shape, y_shape), (y_spec, y_spec)
    big = pltpu.VMEM((tb, WIDTH), F32)
    nc = tb // RW_CHUNK
    pair_mats = pltpu.VMEM((nc, N_PAIRS, LANES, LANES), F32)
    scratch = [pltpu.VMEM((8, 3 * WIDTH), F32), pltpu.VMEM((8, 256), F32), pltpu.VMEM((8, 128), F32),
               pltpu.VMEM((N_PAIRS, LANES, LANES), F32), pair_mats, pair_mats,
               pltpu.VMEM((nc, 8, WIDTH), F32)] + [big] * 10
    return pl.pallas_call(
        functools.partial(_rwkv_kernel, has_vres=has_vres, tb=tb, group=group),
        out_shape=out_shape, grid=(batch, nt), in_specs=in_specs, out_specs=out_specs,
        scratch_shapes=scratch, compiler_params=_params("parallel", "arbitrary"),
        name="rwkv7_vres" if has_vres else "rwkv7",
    )(*args)


def _ssd_kernel(zx_ref, dt_ref, convw, convb, dtb, a_neg, dskip, normw, exp_ch, exp_ln,
                y_ref, c_conv, st_ref, *, nb):
    q = SSM_CHUNK
    rows_all = nb * q

    @pl.when(pl.program_id(1) == 0)
    def _():
        c_conv[...] = jnp.zeros_like(c_conv)
        st_ref[...] = jnp.zeros_like(st_ref)

    zx = zx_ref[...].astype(F32)
    z_all = zx[:, :WIDTH]
    xbc = zx[:, WIDTH:]
    carry = c_conv[...]
    conv = xbc * convw[SSM_CONV - 1:SSM_CONV, :] + convb[...]
    for j in range(1, SSM_CONV):
        conv = conv + _shift_rows(carry, xbc, j) * convw[SSM_CONV - 1 - j:SSM_CONV - j, :]
    c_conv[...] = xbc[rows_all - 8:]
    xbc = conv * _sigmoid(conv)
    dt_all = _softplus(dt_ref[...].astype(F32) + dtb[...])
    a_all = dt_all * a_neg[...]
    row = lax.broadcasted_iota(jnp.int32, (q, q), 0)
    col = lax.broadcasted_iota(jnp.int32, (q, q), 1)
    causal = row >= col
    tri = jnp.where(causal, 1.0, 0.0).astype(BF16)
    ech = exp_ch[...]
    eln = exp_ln[...]
    lo = lax.broadcasted_iota(jnp.int32, (q, LANES), 1) < HEAD_DIM
    per_g = N_HEADS // SSM_GROUPS
    gw = per_g * HEAD_DIM
    states = [st_ref[g] for g in range(SSM_GROUPS)]
    for sb in range(nb):
        rs = slice(sb * q, (sb + 1) * q)
        xs = xbc[rs, :WIDTH]
        bm = xbc[rs, WIDTH:WIDTH + SSM_GROUPS * SSM_STATE]
        cm = xbc[rs, WIDTH + SSM_GROUPS * SSM_STATE:]
        dt = dt_all[rs]
        acum = _dot01_left(tri, a_all[rs])
        acum_t = acum.T
        dt_x = _dot_split(dt, ech)
        acum_x = _dot01_right(acum, ech)
        atot_x = acum_x[q - 1:q, :]
        acum_l = _dot01_right(acum, eln)
        xdt = xs * dt_x
        xdec = xdt * jnp.exp(atot_x - acum_x)
        e_in = jnp.exp(acum_x)
        e_tot = jnp.exp(atot_x)
        ys = []
        for g in range(SSM_GROUPS):
            bg = bm[:, g * SSM_STATE:(g + 1) * SSM_STATE]
            cg = cm[:, g * SSM_STATE:(g + 1) * SSM_STATE]
            cb = _dot_nt(cg, bg)
            gl = slice(g * gw, (g + 1) * gw)
            y_g = _dot(cg, states[g]) * e_in[:, gl]
            states[g] = states[g] * e_tot[:, gl] + _dot(bg.T, xdec[:, gl])
            diag = []
            for pp in range(per_g // 2):
                xp = xdt[:, gl][:, pp * LANES:(pp + 1) * LANES]
                acc = None
                for hh in range(2):
                    h = g * per_g + 2 * pp + hh
                    diff = acum_l[:, h * LANES:(h + 1) * LANES] - acum_t[h:h + 1, :]
                    lmat = jnp.where(causal, jnp.exp(jnp.where(causal, diff, 0.0)), 0.0)
                    xh = jnp.where(lo, xp, 0.0) if hh == 0 else jnp.where(lo, 0.0, xp)
                    t = _dot(cb * lmat, xh)
                    acc = t if acc is None else acc + t
                diag.append(acc)
            ys.append(y_g + jnp.concatenate(diag, axis=1))
        y = jnp.concatenate(ys, axis=1) + xs * dskip[...]
        z = z_all[rs]
        yz = y * (z * _sigmoid(z))
        outs = []
        for g in range(SSM_GROUPS):
            yg = yz[:, g * gw:(g + 1) * gw]
            outs.append(yg * lax.rsqrt(jnp.mean(yg * yg, axis=-1, keepdims=True) + NORM_EPS))
        y_ref[rs, :] = (jnp.concatenate(outs, axis=1) * normw[...]).astype(BF16)
    for g in range(SSM_GROUPS):
        st_ref[g] = states[g]


def ssd_mixer(proj, prm, l, batch, seq, nb=4):
    n = proj.shape[0]
    rows = nb * SSM_CHUNK
    nt = seq // rows
    row = lambda b, t: b * nt + t
    names = ["convw", "convb", "dtb", "a_neg", "dskip", "normw", "exp_ch", "exp_ln"]
    specs, args = _layer_specs(prm, names, l)
    in_specs = [pl.BlockSpec((rows, 3 * WIDTH), lambda b, t: (row(b, t), C_SSM // (3 * WIDTH))),
                pl.BlockSpec((rows, 128), lambda b, t: (row(b, t), C_DT // 128))] + specs
    return pl.pallas_call(
        functools.partial(_ssd_kernel, nb=nb),
        out_shape=jax.ShapeDtypeStruct((n, WIDTH), BF16),
        grid=(batch, nt), in_specs=in_specs,
        out_specs=pl.BlockSpec((rows, WIDTH), lambda b, t: (row(b, t), 0)),
        scratch_shapes=[pltpu.VMEM((8, WIDTH + 2 * SSM_GROUPS * SSM_STATE), F32),
                        pltpu.VMEM((SSM_GROUPS, SSM_STATE, WIDTH // SSM_GROUPS), F32)],
        compiler_params=_params("parallel", "arbitrary"),
        name="ssd",
    )(proj, proj, *args)


def _swa_kernel(sink_ref, qkv_ref, y_ref, kv_prev, *, nb, layer):
    wdw = WINDOW

    @pl.when(pl.program_id(1) == 0)
    def _():
        kv_prev[...] = jnp.zeros_like(kv_prev)

    qkv = qkv_ref[...].astype(F32)
    qi = lax.broadcasted_iota(jnp.int32, (wdw, 2 * wdw), 0)
    kj = lax.broadcasted_iota(jnp.int32, (wdw, 2 * wdw), 1)
    rel = qi + wdw - kj
    in_window = (rel >= 0) & (rel < wdw)
    first = jnp.where(pl.program_id(1) > 0, 0, wdw)
    lo = lax.broadcasted_iota(jnp.int32, (wdw, LANES), 1) < HEAD_DIM
    lo2 = lax.broadcasted_iota(jnp.int32, (2 * wdw, LANES), 1) < HEAD_DIM
    rep = N_HEADS // ATT_KV_HEADS
    scale = HEAD_DIM ** -0.5
    bands, scores = [], []
    prev = kv_prev[...]
    for sb in range(nb):
        rs = slice(sb * wdw, (sb + 1) * wdw)
        kv = qkv[rs, WIDTH:]
        band = jnp.concatenate([prev, kv], axis=0)
        prev = kv
        bands.append(band)
        valid = in_window & (kj >= first) if sb == 0 else in_window
        for h in range(N_HEADS):
            g = h // rep
            qp = qkv[rs, (h // 2) * LANES:(h // 2 + 1) * LANES]
            qh = jnp.where(lo, qp, 0.0) if h % 2 == 0 else jnp.where(lo, 0.0, qp)
            s = _dot_nt(qh, band[:, g * LANES:(g + 1) * LANES]) * scale
            scores.append(jnp.where(valid, s, -1e30))
    kv_prev[...] = prev
    probs = []
    for i, s in enumerate(scores):
        sink = sink_ref[layer, i % N_HEADS]
        m = jnp.maximum(jnp.max(s, axis=-1, keepdims=True), sink)
        p = jnp.exp(s - m)
        den = jnp.sum(p, axis=-1, keepdims=True) + jnp.exp(sink - m)
        probs.append(p / den)
    for sb in range(nb):
        outs = []
        for tile in range(N_HEADS // 2):
            g = (2 * tile) // rep
            vb = bands[sb][:, (ATT_KV_HEADS + g) * LANES:(ATT_KV_HEADS + g + 1) * LANES]
            outs.append(_dot(probs[sb * N_HEADS + 2 * tile], jnp.where(lo2, vb, 0.0))
                        + _dot(probs[sb * N_HEADS + 2 * tile + 1], jnp.where(lo2, 0.0, vb)))
        y_ref[sb * wdw:(sb + 1) * wdw, :] = jnp.concatenate(outs, axis=1).astype(BF16)


def swa_mixer(proj, sinks, l, batch, seq, nb=4):
    n = proj.shape[0]
    rows = nb * WINDOW
    nt = seq // rows
    row = lambda b, t: b * nt + t
    return pl.pallas_call(
        functools.partial(_swa_kernel, nb=nb, layer=l),
        out_shape=jax.ShapeDtypeStruct((n, WIDTH), BF16),
        grid=(batch, nt),
        in_specs=[pl.BlockSpec(memory_space=pltpu.SMEM),
                  pl.BlockSpec((rows, 2 * WIDTH), lambda b, t: (row(b, t), C_ATT // (2 * WIDTH)))],
        out_specs=pl.BlockSpec((rows, WIDTH), lambda b, t: (row(b, t), 0)),
        scratch_shapes=[pltpu.VMEM((WINDOW, WIDTH), F32)],
        compiler_params=_params("parallel", "arbitrary"),
        name="swa",
    )(sinks, proj)


def _merge_ffn_kernel(x_ref, yrw, yssm, yatt, gmix, wgate, gb, wrw, wssm, watt, wout, gamma, wgu, wd,
                      gfin, o_ref, act, *, chunk, final_norm):
    x_in = x_ref[...]
    xm = _rms(x_in, gmix[...]).astype(BF16)
    merged = None
    for i, (y, w) in enumerate(((yrw, wrw), (yssm, wssm), (yatt, watt))):
        cols = slice(i * D_MODEL, (i + 1) * D_MODEL)
        gate = _sigmoid(jnp.dot(xm, wgate[:, cols], preferred_element_type=F32) + gb[:, cols])
        t = gate * jnp.dot(y[...], w[...], preferred_element_type=F32)
        merged = t if merged is None else merged + t
    x = x_in + _dot(merged, wout[...])
    xn = _rms(x, gamma[...]).astype(BF16)
    for c0 in range(0, FFN_HIDDEN, chunk):
        gate = jnp.dot(xn, wgu[:, c0:c0 + chunk], preferred_element_type=F32)
        up = jnp.dot(xn, wgu[:, FFN_HIDDEN + c0:FFN_HIDDEN + c0 + chunk], preferred_element_type=F32)
        act[:, c0:c0 + chunk] = (gate * _sigmoid(gate) * up).astype(BF16)
    out = x + jnp.dot(act[...], wd[...], preferred_element_type=F32)
    if final_norm:
        out = _rms(out, gfin[...])
    o_ref[...] = out


def merge_ffn(x, y_rw, y_ssm, y_att, prm, l, final_norm, tm=512, chunk=256):
    n = x.shape[0]
    names = ["norm_mix", "w_gate", "gate_b", "w_br_rw", "w_br_ssm", "w_br_att", "w_out", "norm_ffn",
             "w_gu", "w_down"]
    specs, args = _layer_specs(prm, names, l)
    fspecs, fargs = _layer_specs(prm, ["norm_final"], 0)
    rows = lambda width: pl.BlockSpec((tm, width), lambda i: (i, 0))
    return pl.pallas_call(
        functools.partial(_merge_ffn_kernel, chunk=chunk, final_norm=final_norm),
        out_shape=jax.ShapeDtypeStruct((n, D_MODEL), F32),
        grid=(n // tm,),
        in_specs=[rows(D_MODEL), rows(WIDTH), rows(WIDTH), rows(WIDTH)] + specs + fspecs,
        out_specs=rows(D_MODEL),
        scratch_shapes=[pltpu.VMEM((tm, FFN_HIDDEN), BF16)],
        compiler_params=_params("parallel"),
        name="merge_ffn_final" if final_norm else "merge_ffn",
    )(x, y_rw, y_ssm, y_att, *args, *fargs)


def _pad_axis(w, size, axis, at=0):
    pads = [(0, 0)] * w.ndim
    pads[axis] = (at, size - at - w.shape[axis])
    return jnp.pad(w, pads)


def _prep_params(p):
    depth = p["w_in"].shape[0]
    rw_cols = 3 * WIDTH + RW_DECAY_LORA + RW_ICLR_LORA + RW_GATE_LORA
    ssm_cols = WIDTH + (WIDTH + 2 * SSM_GROUPS * SSM_STATE) + N_HEADS
    att_cols = WIDTH + 2 * ATT_KV_HEADS * HEAD_DIM
    o_ssm = rw_cols
    o_att = o_ssm + ssm_cols
    o_gate = o_att + att_cols
    o_k = o_att + WIDTH
    o_v = o_k + ATT_KV_HEADS * HEAD_DIM
    w = p["w_in"].astype(BF16)
    zeros = lambda c: jnp.zeros((depth, D_MODEL, c), w.dtype)
    dup = lambda o: [w[:, :, o + HEAD_DIM * (i // 2):o + HEAD_DIM * (i // 2 + 1)] for i in range(4)]
    lv = jnp.concatenate([zeros(RW_VRES_LORA)[:1], p["rw_vres_down"].astype(BF16)], axis=0)
    pad_lglv = 256 - RW_GATE_LORA - RW_VRES_LORA
    w_cat = jnp.concatenate(
        [w[:, :, :3 * WIDTH],
         w[:, :, o_ssm:o_ssm + 3 * WIDTH],
         w[:, :, o_att:o_att + WIDTH]] + dup(o_k) + dup(o_v)
        + [w[:, :, 3 * WIDTH + 128:rw_cols], lv, zeros(pad_lglv),
           w[:, :, 3 * WIDTH:3 * WIDTH + 128],
           w[:, :, o_ssm + 3 * WIDTH:o_ssm + 3 * WIDTH + N_HEADS], zeros(128 - N_HEADS)], axis=2)
    assert w_cat.shape == (depth, D_MODEL, IN_COLS_PAD)

    row = lambda v: v.reshape(v.shape[0], 1, -1).astype(F32)
    mu = p["rw_mu"]
    vmu = jnp.concatenate([jnp.zeros((1, RW_VRES_LORA), F32), p["rw_vres_mu"]], axis=0)
    head_of = jnp.arange(WIDTH) // HEAD_DIM
    return {
        "norm_mix": row(p["norm_mix"]),
        "w_cat": w_cat,
        "w_gate": w[:, :, o_gate:o_gate + 3 * D_MODEL],
        "mu_rkv": row(mu[:, :3 * WIDTH]),
        "mu_lwla": row(mu[:, 3 * WIDTH:3 * WIDTH + 128]),
        "mu_lglv": row(jnp.concatenate([mu[:, 3 * WIDTH + 128:], vmu,
                                        jnp.zeros((depth, pad_lglv), F32)], axis=1)),
        "w0": row(p["rw_w0"]),
        "w2p": _pad_axis(p["rw_w2"], 128, 1, 0).astype(BF16),
        "a0": row(p["rw_a0"]),
        "a2p": _pad_axis(p["rw_a2"], 128, 1, RW_DECAY_LORA).astype(BF16),
        "g2p": _pad_axis(p["rw_g2"], 256, 1, 0).astype(BF16),
        "kkw": row(p["rw_k_k"]), "kaw": row(p["rw_k_a"]), "rkw": row(p["rw_r_k"]),
        "gnw": row(p["rw_gn_w"]), "gnb": row(p["rw_gn_b"]),
        "v0": row(p["rw_vres_v0"]),
        "v2p": _pad_axis(p["rw_vres_v2"], 256, 1, RW_GATE_LORA).astype(BF16),
        "hsum": (head_of[:, None] == head_of[None, :]).astype(BF16)[None],
        "convw": p["ssm_conv_w"].astype(F32),
        "convb": row(p["ssm_conv_b"]),
        "dtb": row(_pad_axis(p["ssm_dt_bias"], 128, 1)),
        "a_neg": row(_pad_axis(-jnp.exp(p["ssm_a_log"].astype(F32)), 128, 1)),
        "dskip": row(jnp.repeat(p["ssm_d"], HEAD_DIM, axis=1)),
        "normw": row(p["ssm_norm_w"]),
        "exp_ch": (jnp.arange(128)[:, None] == head_of[None, :]).astype(BF16)[None],
        "exp_ln": (jnp.arange(128)[:, None]
                   == (jnp.arange(N_HEADS * LANES) // LANES)[None, :]).astype(BF16)[None],
        "sinks": p["att_sinks"].astype(F32),
        "gate_b": row(p["gate_b"]),
        "w_br_rw": p["w_br_rw"].astype(BF16),
        "w_br_ssm": p["w_br_ssm"].astype(BF16),
        "w_br_att": p["w_br_att"].astype(BF16),
        "w_out": p["w_out"].astype(BF16),
        "norm_ffn": row(p["norm_ffn"]),
        "w_gu": p["ffn_w_gu"].astype(BF16),
        "w_down": p["ffn_w_down"].astype(BF16),
        "norm_final": p["norm_final"].reshape(1, 1, -1).astype(F32),
    }


def kernel(x, norm_mix, w_in, rw_mu, rw_w0, rw_w2, rw_a0, rw_a2, rw_g2, rw_k_k, rw_k_a, rw_r_k, rw_gn_w, rw_gn_b, rw_vres_down, rw_vres_mu, rw_vres_v0, rw_vres_v2, ssm_conv_w, ssm_conv_b, ssm_dt_bias, ssm_a_log, ssm_d, ssm_norm_w, att_sinks, gate_b, w_br_rw, w_br_ssm, w_br_att, w_out, norm_ffn, ffn_w_gu, ffn_w_down, norm_final):
    p = dict(norm_mix=norm_mix, w_in=w_in, rw_mu=rw_mu, rw_w0=rw_w0, rw_w2=rw_w2, rw_a0=rw_a0,
             rw_a2=rw_a2, rw_g2=rw_g2, rw_k_k=rw_k_k, rw_k_a=rw_k_a,
             rw_r_k=rw_r_k.reshape(rw_r_k.shape[0], -1), rw_gn_w=rw_gn_w, rw_gn_b=rw_gn_b,
             rw_vres_down=rw_vres_down, rw_vres_mu=rw_vres_mu, rw_vres_v0=rw_vres_v0,
             rw_vres_v2=rw_vres_v2, ssm_conv_w=ssm_conv_w, ssm_conv_b=ssm_conv_b,
             ssm_dt_bias=ssm_dt_bias, ssm_a_log=ssm_a_log, ssm_d=ssm_d, ssm_norm_w=ssm_norm_w,
             att_sinks=att_sinks, gate_b=gate_b, w_br_rw=w_br_rw, w_br_ssm=w_br_ssm,
             w_br_att=w_br_att, w_out=w_out, norm_ffn=norm_ffn, ffn_w_gu=ffn_w_gu,
             ffn_w_down=ffn_w_down, norm_final=norm_final)
    batch, seq, _ = x.shape
    depth = w_in.shape[0]
    prm = _prep_params(p)
    xf = x.reshape(batch * seq, D_MODEL)
    vfirst = None
    for l in range(depth):
        proj = in_proj(xf, prm, l)
        if l == 0:
            y_rw, vfirst = rwkv_mixer(proj, None, prm, l, batch, seq)
        else:
            y_rw = rwkv_mixer(proj, vfirst, prm, l, batch, seq)
        y_ssm = ssd_mixer(proj, prm, l, batch, seq)
        y_att = swa_mixer(proj, prm["sinks"], l, batch, seq)
        xf = merge_ffn(xf, y_rw, y_ssm, y_att, prm, l, final_norm=(l == depth - 1))
    return xf.reshape(batch, seq, D_MODEL)
```

```python
import functools

import jax
import jax.numpy as jnp
from jax import lax
from jax.experimental import pallas as pl
from jax.experimental.pallas import tpu as pltpu

F32 = jnp.float32
BF16 = jnp.bfloat16

D_MODEL = 1024
HEAD_DIM = 64
N_HEADS = 8
WIDTH = N_HEADS * HEAD_DIM
LANES = 128
N_PAIRS = WIDTH // LANES
RW_CHUNK = 64
RW_DECAY_LORA = 64
RW_ICLR_LORA = 64
RW_VRES_LORA = 32
RW_GATE_LORA = 160
RW_GN_EPS = 64e-5
RW_DECAY_SCALE = 0.6065306597126334
SSM_STATE = 128
SSM_GROUPS = 2
SSM_CHUNK = 128
SSM_CONV = 4
ATT_KV_HEADS = 2
WINDOW = 128
FFN_HIDDEN = 2816
NORM_EPS = 1e-6
VMEM_LIMIT = 56 * 1024 * 1024

C_RKV = 0
C_SSM = 1536
C_ATT = 3072
C_LGLV = 4096
C_LWLA = 4352
C_DT = 4480
IN_COLS_PAD = 4608


def _dot(a, b):
    return jnp.dot(a.astype(BF16), b.astype(BF16), preferred_element_type=F32)


def _dot_nt(a, b):
    return lax.dot_general(a.astype(BF16), b.astype(BF16), (((1,), (1,)), ((), ())),
                           preferred_element_type=F32)


def _split3(x):
    hi = x.astype(BF16)
    r1 = x - hi.astype(F32)
    mid = r1.astype(BF16)
    return hi, mid, (r1 - mid.astype(F32)).astype(BF16)


def _dot01_left(m01, x):
    return sum(jnp.dot(m01, part, preferred_element_type=F32) for part in _split3(x))


def _dot01_right(x, m01):
    return sum(jnp.dot(part, m01, preferred_element_type=F32) for part in _split3(x))


def _dot_split(a, b01):
    hi = a.astype(BF16)
    lo = (a - hi.astype(F32)).astype(BF16)
    return (jnp.dot(hi, b01, preferred_element_type=F32)
            + jnp.dot(lo, b01, preferred_element_type=F32))


def _sigmoid(x):
    return 1.0 / (1.0 + jnp.exp(-x))


def _softplus(x):
    return jnp.maximum(x, 0.0) + jnp.log(1.0 + jnp.exp(-jnp.abs(x)))


def _rms(x, w):
    return x * lax.rsqrt(jnp.mean(x * x, axis=-1, keepdims=True) + NORM_EPS) * w


def _shift_rows(carry8, p, j):
    ext = jnp.concatenate([carry8, p], axis=0)
    return pltpu.roll(ext, j, 0)[8:]


def _params(*sem):
    return pltpu.CompilerParams(dimension_semantics=sem, vmem_limit_bytes=VMEM_LIMIT)


_SHARED = ("hsum", "exp_ch", "exp_ln")
_FROM_SECOND_LAYER = ("v0", "v2p")


def _layer_specs(prm, names, l):
    specs, args = [], []
    for nm in names:
        arr = prm[nm]
        idx = 0 if nm in _SHARED else (l - 1 if nm in _FROM_SECOND_LAYER else l)
        nd = arr.ndim
        specs.append(pl.BlockSpec((None,) + arr.shape[1:],
                                  lambda *_, idx=idx, nd=nd: (idx,) + (0,) * (nd - 1),
                                  pipeline_mode=pl.Buffered(1)))
        args.append(arr)
    return specs, args


def _in_proj_kernel(x_ref, g_ref, w_ref, o_ref, *, chunk):
    xn = _rms(x_ref[...], g_ref[...]).astype(BF16)
    for c0 in range(0, IN_COLS_PAD, chunk):
        o_ref[:, c0:c0 + chunk] = jnp.dot(
            xn, w_ref[:, c0:c0 + chunk], preferred_element_type=F32).astype(BF16)


def in_proj(x, prm, l, tm=512, chunk=512):
    n = x.shape[0]
    specs, args = _layer_specs(prm, ["norm_mix", "w_cat"], l)
    return pl.pallas_call(
        functools.partial(_in_proj_kernel, chunk=chunk),
        out_shape=jax.ShapeDtypeStruct((n, IN_COLS_PAD), BF16),
        grid=(n // tm,),
        in_specs=[pl.BlockSpec((tm, D_MODEL), lambda i: (i, 0))] + specs,
        out_specs=pl.BlockSpec((tm, IN_COLS_PAD), lambda i: (i, 0)),
        compiler_params=_params("parallel"),
        name="in_proj",
    )(x, *args)


def _split_bf16(x):
    hi = x.astype(BF16)
    return hi, (x - hi.astype(F32)).astype(BF16)


def _pair_tiles(xb, bd):
    return [jnp.concatenate([xb[:, i:i + LANES]] * 2, axis=0) * bd
            for i in range(0, xb.shape[1], LANES)]


def _mm_tiles(lhs, tiles):
    stacked = lhs[0] if len(lhs) == 1 else jnp.concatenate(lhs, axis=0)
    return jnp.concatenate(
        [jnp.dot(stacked[:, p * LANES:(p + 1) * LANES], t, preferred_element_type=F32)
         for p, t in enumerate(tiles)], axis=1)


def _mm_split(a_hl, b_tiles_hl, m):
    r1 = _mm_tiles(list(a_hl), b_tiles_hl[0])
    return r1[:m] + r1[m:] + _mm_tiles([a_hl[0]], b_tiles_hl[1])


def _headwise_mm(a, b, bd):
    return _mm_tiles([a.astype(BF16)], _pair_tiles(b.astype(BF16), bd))


def _tri_inverse(n_cat, bd):
    size, width = n_cat.shape
    row = lax.broadcasted_iota(jnp.int32, (size, LANES), 0)
    col = lax.broadcasted_iota(jnp.int32, (size, LANES), 1) % size
    wide = lambda tile: jnp.concatenate([tile] * (width // LANES), axis=1)

    def same_block(b):
        return (row // b) == (col // b)

    a0 = jnp.where(wide(same_block(8)), n_cat, 0.0).astype(BF16)
    d0 = wide(jnp.where(row == col, 1.0, 0.0).astype(BF16)) + a0
    p = _mm_tiles([a0], _pair_tiles(a0, bd))
    p_hl = _split_bf16(p)
    p_tiles = (_pair_tiles(p_hl[0], bd), _pair_tiles(p_hl[1], bd))
    r1 = _mm_tiles([d0, p_hl[0], p_hl[1]], p_tiles[0])
    r2 = _mm_tiles([d0, p_hl[0]], p_tiles[1])
    d = d0.astype(F32) + r1[:size] + r2[:size]
    p = r1[size:2 * size] + r1[2 * size:] + r2[size:]
    p_hl = _split_bf16(p)
    d = d + _mm_split(_split_bf16(d), (_pair_tiles(p_hl[0], bd), _pair_tiles(p_hl[1], bd)), size)
    b = 8
    while b < size:
        e = jnp.where(wide(same_block(2 * b) & jnp.logical_not(same_block(b))), n_cat, 0.0).astype(BF16)
        d_hl = _split_bf16(d)
        r = _mm_tiles(list(d_hl), _pair_tiles(e, bd))
        x_hl = _split_bf16(r[:size] + r[size:])
        d = d + _mm_split(x_hl, (_pair_tiles(d_hl[0], bd), _pair_tiles(d_hl[1], bd)), size)
        b *= 2
    return d


def _rwkv_kernel(*refs, has_vres, tb, group):
    if has_vres:
        (rkv_ref, lglv_ref, lwla_ref, vfirst_ref, mu_rkv, mu_lglv, mu_lwla, w0, w2p, a0, a2p,
         g2p, kkw, kaw, rkw, gnw, gnb, hsum, v0, v2p,
         y_ref,
         c_rkv, c_lglv, c_lwla, s_ref, phi_s, psi_s, etot_s,
         r_s, k_s, v_s, kk_s, kka_s, lw_s, g_s, y_s, rhat_s, y0_s) = refs
    else:
        (rkv_ref, lglv_ref, lwla_ref, mu_rkv, mu_lglv, mu_lwla, w0, w2p, a0, a2p,
         g2p, kkw, kaw, rkw, gnw, gnb, hsum,
         y_ref, vfirst_out,
         c_rkv, c_lglv, c_lwla, s_ref, phi_s, psi_s, etot_s,
         r_s, k_s, v_s, kk_s, kka_s, lw_s, g_s, y_s, rhat_s, y0_s) = refs

    @pl.when(pl.program_id(1) == 0)
    def _():
        c_rkv[...] = jnp.zeros_like(c_rkv)
        c_lglv[...] = jnp.zeros_like(c_lglv)
        c_lwla[...] = jnp.zeros_like(c_lwla)
        s_ref[...] = jnp.zeros_like(s_ref)

    def mix(p_ref, carry, mu):
        p = p_ref[...].astype(F32)
        prev = _shift_rows(carry[...], p, 1)
        carry[...] = p[tb - 8:]
        return p + (prev - p) * mu[...]

    rkv = mix(rkv_ref, c_rkv, mu_rkv)
    lglv = mix(lglv_ref, c_lglv, mu_lglv)
    lwla = mix(lwla_ref, c_lwla, mu_lwla)
    r = rkv[:, :WIDTH]
    k = rkv[:, WIDTH:2 * WIDTH]
    v = rkv[:, 2 * WIDTH:]

    lw_s[...] = -_sigmoid(w0[...] + _dot(jnp.tanh(lwla), w2p[...])) * RW_DECAY_SCALE
    a = _sigmoid(a0[...] + _dot(lwla, a2p[...]))
    g_s[...] = _dot(_sigmoid(lglv), g2p[...])
    if has_vres:
        v = v + (vfirst_ref[...].astype(F32) - v) * _sigmoid(v0[...] + _dot(lglv, v2p[...]))
    else:
        vfirst_out[...] = v.astype(BF16)
    hs = hsum[...]
    kk = k * kkw[...]
    kk = kk * lax.rsqrt(jnp.maximum(_dot(kk * kk, hs), 1e-24))
    k = k * (1.0 + (a - 1.0) * kaw[...])
    r_s[...] = r
    k_s[...] = k
    v_s[...] = v
    kk_s[...] = kk
    kka_s[...] = kk * a

    cl = RW_CHUNK
    n_chunks = tb // cl
    row = lax.broadcasted_iota(jnp.int32, (cl, cl), 0)
    col = lax.broadcasted_iota(jnp.int32, (cl, cl), 1)
    tri_incl = jnp.where(row >= col, 1.0, 0.0).astype(BF16)
    row2 = lax.broadcasted_iota(jnp.int32, (2 * cl, 2 * LANES), 0)
    col2 = lax.broadcasted_iota(jnp.int32, (2 * cl, 2 * LANES), 1)
    m_mask = (row2 % cl - col2 % cl) >= jnp.where(row2 < cl, 1, 0)
    rr = lax.broadcasted_iota(jnp.int32, (LANES, LANES), 0)
    cc = lax.broadcasted_iota(jnp.int32, (LANES, LANES), 1)
    same_head = (rr // HEAD_DIM) == (cc // HEAD_DIM)
    lane_lo = cc < HEAD_DIM
    bd = jnp.where(same_head, 1.0, 0.0).astype(BF16)
    same_head_w = jnp.concatenate([same_head, same_head], axis=1)
    same_head_t = jnp.concatenate([same_head, same_head], axis=0)
    zeros_half = jnp.zeros((cl, LANES), F32)

    def group_pre(gi, carry):
        chunks = [gi * group + g for g in range(group)]
        rows = [pl.ds(pl.multiple_of(ci * cl, cl), cl) for ci in chunks]
        lws = [lw_s[r, :] for r in rows]
        cums = [_dot01_left(tri_incl, lw) for lw in lws]
        rts, ats, bts, kts, e_mids, e_ends = [], [], [], [], [], []
        for g in range(group):
            cum, lw = cums[g], lws[g]
            mid = cum[cl // 2 - 1:cl // 2, :]
            cen = cum - mid
            e_pos = jnp.exp(cen)
            e_neg = jnp.exp(-cen)
            rts.append(r_s[rows[g], :] * e_pos)
            ats.append(-kk_s[rows[g], :] * jnp.exp(cen - lw))
            bts.append(kka_s[rows[g], :] * e_neg)
            kts.append(k_s[rows[g], :] * e_neg)
            e_mids.append(jnp.exp(mid))
            e_ends.append(jnp.exp(cen[cl - 1:cl, :]))
            etot_s[chunks[g]] = jnp.broadcast_to(jnp.exp(cum[cl - 1:cl, :]), (8, WIDTH))
        pairs = [(g, p) for g in range(group) for p in range(N_PAIRS)]
        lane = lambda p: slice(p * LANES, (p + 1) * LANES)
        zs = [jnp.concatenate([bts[g][:, lane(p)], kts[g][:, lane(p)]], axis=0)
              for g, p in pairs]
        zts = [z.T for z in zs]
        ms = []
        for (g, p), zt in zip(pairs, zts):
            zr = pltpu.roll(zt, HEAD_DIM, 1)
            wgt = jnp.concatenate([jnp.where(lane_lo, zt, zr), jnp.where(lane_lo, zr, zt)], axis=1)
            wgt = jnp.where(same_head_w, wgt, 0.0)
            m = _dot(jnp.concatenate([ats[g][:, lane(p)], rts[g][:, lane(p)]], axis=0), wgt)
            ms.append(jnp.where(m_mask, m, 0.0))
        rt = jnp.concatenate(rts, axis=1)
        at = jnp.concatenate(ats, axis=1)
        vv = jnp.concatenate([v_s[r, :] for r in rows], axis=1)
        e_mid = jnp.concatenate(e_mids, axis=1)
        a_ab = jnp.concatenate([m[:cl, :LANES] for m in ms], axis=1)
        a_kr = jnp.concatenate(
            [jnp.concatenate([m[:cl, LANES:] for m in ms], axis=1),
             jnp.concatenate([m[cl:, LANES:] for m in ms], axis=1)], axis=0)
        a_rb = jnp.concatenate([m[cl:, :LANES] for m in ms], axis=1).astype(BF16)
        t_hl = _split_bf16(_tri_inverse(a_ab, bd))
        akv = _headwise_mm(a_kr, vv, bd)
        at_hl = _split_bf16(at)
        av_hl = _split_bf16(akv[:cl])
        at_t = _mm_split(t_hl, (_pair_tiles(at_hl[0], bd), _pair_tiles(at_hl[1], bd)), cl)
        u0 = _mm_split(t_hl, (_pair_tiles(av_hl[0], bd), _pair_tiles(av_hl[1], bd)), cl)
        rhat = (rt + _mm_tiles([a_rb], _pair_tiles(at_t.astype(BF16), bd))) * e_mid
        y0 = _mm_tiles([a_rb], _pair_tiles(u0.astype(BF16), bd)) + akv[cl:]
        at_m = at_t * e_mid
        for g in range(group):
            rhat_s[rows[g], :] = rhat[:, g * WIDTH:(g + 1) * WIDTH]
            y0_s[rows[g], :] = y0[:, g * WIDTH:(g + 1) * WIDTH]
        for i, (g, p) in enumerate(pairs):
            ln = slice(g * WIDTH + p * LANES, g * WIDTH + (p + 1) * LANES)
            lhs = jnp.concatenate(
                [jnp.concatenate([at_m[:, ln], zeros_half], axis=0),
                 jnp.concatenate([u0[:, ln], vv[:, ln]], axis=0)], axis=1)
            pp = _dot(lhs.T, zs[i]) * e_ends[g][:, lane(p)]
            pp = jnp.where(same_head_t, pp, 0.0)
            phi_s[chunks[g], p] = pp[:LANES]
            psi_s[chunks[g], p] = pp[LANES:]
        return carry

    if n_chunks == group:
        group_pre(0, 0)
    else:
        lax.fori_loop(0, n_chunks // group, group_pre, 0)

    states = [s_ref[p] for p in range(N_PAIRS)]
    for ci in range(n_chunks):
        rows = slice(ci * cl, (ci + 1) * cl)
        et = etot_s[ci][0:1]
        for p in range(N_PAIRS):
            ln = slice(p * LANES, (p + 1) * LANES)
            s0 = states[p]
            y_s[rows, ln] = _dot_nt(rhat_s[rows, ln], s0) + y0_s[rows, ln]
            states[p] = s0 * et[:, ln] + _dot(s0, phi_s[ci, p]) + psi_s[ci, p]
    for p in range(N_PAIRS):
        s_ref[p] = states[p]

    y = y_s[...]
    inv_n = 1.0 / HEAD_DIM
    mean = _dot(y, hs) * inv_n
    yc = y - mean
    var = _dot(yc * yc, hs) * inv_n
    yn = yc * lax.rsqrt(var + RW_GN_EPS) * gnw[...] + gnb[...]
    bonus = _dot(r_s[...] * k_s[...] * rkw[...], hs) * v_s[...]
    y_ref[...] = ((yn + bonus) * g_s[...]).astype(BF16)


def rwkv_mixer(proj, vfirst, prm, l, batch, seq, tb=512, group=8):
    n = proj.shape[0]
    nt = seq // tb
    has_vres = vfirst is not None
    row = lambda b, t: b * nt + t
    in_specs = [pl.BlockSpec((tb, 3 * WIDTH), lambda b, t: (row(b, t), C_RKV // (3 * WIDTH))),
                pl.BlockSpec((tb, 256), lambda b, t: (row(b, t), C_LGLV // 256)),
                pl.BlockSpec((tb, 128), lambda b, t: (row(b, t), C_LWLA // 128))]
    args = [proj, proj, proj]
    if has_vres:
        in_specs.append(pl.BlockSpec((tb, WIDTH), lambda b, t: (row(b, t), 0)))
        args.append(vfirst)
    names = ["mu_rkv", "mu_lglv", "mu_lwla", "w0", "w2p", "a0", "a2p", "g2p", "kkw", "kaw", "rkw",
             "gnw", "gnb", "hsum"]
    if has_vres:
        names += ["v0", "v2p"]
    specs, pargs = _layer_specs(prm, names, l)
    in_specs += specs
    args += pargs
    y_spec = pl.BlockSpec((tb, WIDTH), lambda b, t: (row(b, t), 0))
    y_shape = jax.ShapeDtypeStruct((n, WIDTH), BF16)
    if has_vres:
        out_shape, out_specs = y_shape, y_spec
    else:
        out_shape, out_specs = (y_shape, y_shape), (y_spec, y_spec)
    big = pltpu.VMEM((tb, WIDTH), F32)
    nc = tb // RW_CHUNK
    pair_mats = pltpu.VMEM((nc, N_PAIRS, LANES, LANES), F32)
    scratch = [pltpu.VMEM((8, 3 * WIDTH), F32), pltpu.VMEM((8, 256), F32), pltpu.VMEM((8, 128), F32),
               pltpu.VMEM((N_PAIRS, LANES, LANES), F32), pair_mats, pair_mats,
               pltpu.VMEM((nc, 8, WIDTH), F32)] + [big] * 10
    return pl.pallas_call(
        functools.partial(_rwkv_kernel, has_vres=has_vres, tb=tb, group=group),
        out_shape=out_shape, grid=(batch, nt), in_specs=in_specs, out_specs=out_specs,
        scratch_shapes=scratch, compiler_params=_params("parallel", "arbitrary"),
        name="rwkv7_vres" if has_vres else "rwkv7",
    )(*args)


def _ssd_kernel(zx_ref, dt_ref, convw, convb, dtb, a_neg, dskip, normw, exp_ch, exp_ln,
                y_ref, c_conv, st_ref, *, nb):
    q = SSM_CHUNK
    rows_all = nb * q

    @pl.when(pl.program_id(1) == 0)
    def _():
        c_conv[...] = jnp.zeros_like(c_conv)
        st_ref[...] = jnp.zeros_like(st_ref)

    zx = zx_ref[...].astype(F32)
    z_all = zx[:, :WIDTH]
    xbc = zx[:, WIDTH:]
    carry = c_conv[...]
    conv = xbc * convw[SSM_CONV - 1:SSM_CONV, :] + convb[...]
    for j in range(1, SSM_CONV):
        conv = conv + _shift_rows(carry, xbc, j) * convw[SSM_CONV - 1 - j:SSM_CONV - j, :]
    c_conv[...] = xbc[rows_all - 8:]
    xbc = conv * _sigmoid(conv)
    dt_all = _softplus(dt_ref[...].astype(F32) + dtb[...])
    a_all = dt_all * a_neg[...]
    row = lax.broadcasted_iota(jnp.int32, (q, q), 0)
    col = lax.broadcasted_iota(jnp.int32, (q, q), 1)
    causal = row >= col
    tri = jnp.where(causal, 1.0, 0.0).astype(BF16)
    ech = exp_ch[...]
    eln = exp_ln[...]
    lo = lax.broadcasted_iota(jnp.int32, (q, LANES), 1) < HEAD_DIM
    per_g = N_HEADS // SSM_GROUPS
    gw = per_g * HEAD_DIM
    states = [st_ref[g] for g in range(SSM_GROUPS)]
    for sb in range(nb):
        rs = slice(sb * q, (sb + 1) * q)
        xs = xbc[rs, :WIDTH]
        bm = xbc[rs, WIDTH:WIDTH + SSM_GROUPS * SSM_STATE]
        cm = xbc[rs, WIDTH + SSM_GROUPS * SSM_STATE:]
        dt = dt_all[rs]
        acum = _dot01_left(tri, a_all[rs])
        acum_t = acum.T
        dt_x = _dot_split(dt, ech)
        acum_x = _dot01_right(acum, ech)
        atot_x = acum_x[q - 1:q, :]
        xdt = xs * dt_x
        xdec = xdt * jnp.exp(atot_x - acum_x)
        e_in = jnp.exp(acum_x)
        e_tot = jnp.exp(atot_x)
        ys = []
        for g in range(SSM_GROUPS):
            bg = bm[:, g * SSM_STATE:(g + 1) * SSM_STATE]
            cg = cm[:, g * SSM_STATE:(g + 1) * SSM_STATE]
            cb = _dot_nt(cg, bg)
            gl = slice(g * gw, (g + 1) * gw)
            y_g = _dot(cg, states[g]) * e_in[:, gl]
            states[g] = states[g] * e_tot[:, gl] + _dot(bg.T, xdec[:, gl])
            diag = []
            for pp in range(per_g // 2):
                xp = xdt[:, gl][:, pp * LANES:(pp + 1) * LANES]
                acc = None
                for hh in range(2):
                    h = g * per_g + 2 * pp + hh
                    diff = jnp.broadcast_to(acum[:, h:h + 1], (q, LANES)) - acum_t[h:h + 1, :]
                    lmat = jnp.where(causal, jnp.exp(jnp.where(causal, diff, 0.0)), 0.0)
                    xh = jnp.where(lo, xp, 0.0) if hh == 0 else jnp.where(lo, 0.0, xp)
                    t = _dot(cb * lmat, xh)
                    acc = t if acc is None else acc + t
                diag.append(acc)
            ys.append(y_g + jnp.concatenate(diag, axis=1))
        y = jnp.concatenate(ys, axis=1) + xs * dskip[...]
        z = z_all[rs]
        yz = y * (z * _sigmoid(z))
        outs = []
        for g in range(SSM_GROUPS):
            yg = yz[:, g * gw:(g + 1) * gw]
            outs.append(yg * lax.rsqrt(jnp.mean(yg * yg, axis=-1, keepdims=True) + NORM_EPS))
        y_ref[rs, :] = (jnp.concatenate(outs, axis=1) * normw[...]).astype(BF16)
    for g in range(SSM_GROUPS):
        st_ref[g] = states[g]


def ssd_mixer(proj, prm, l, batch, seq, nb=4):
    n = proj.shape[0]
    rows = nb * SSM_CHUNK
    nt = seq // rows
    row = lambda b, t: b * nt + t
    names = ["convw", "convb", "dtb", "a_neg", "dskip", "normw", "exp_ch", "exp_ln"]
    specs, args = _layer_specs(prm, names, l)
    in_specs = [pl.BlockSpec((rows, 3 * WIDTH), lambda b, t: (row(b, t), C_SSM // (3 * WIDTH))),
                pl.BlockSpec((rows, 128), lambda b, t: (row(b, t), C_DT // 128))] + specs
    return pl.pallas_call(
        functools.partial(_ssd_kernel, nb=nb),
        out_shape=jax.ShapeDtypeStruct((n, WIDTH), BF16),
        grid=(batch, nt), in_specs=in_specs,
        out_specs=pl.BlockSpec((rows, WIDTH), lambda b, t: (row(b, t), 0)),
        scratch_shapes=[pltpu.VMEM((8, WIDTH + 2 * SSM_GROUPS * SSM_STATE), F32),
                        pltpu.VMEM((SSM_GROUPS, SSM_STATE, WIDTH // SSM_GROUPS), F32)],
        compiler_params=_params("parallel", "arbitrary"),
        name="ssd",
    )(proj, proj, *args)


def _swa_kernel(sink_ref, qkv_ref, y_ref, kv_prev, *, nb, layer):
    wdw = WINDOW

    @pl.when(pl.program_id(1) == 0)
    def _():
        kv_prev[...] = jnp.zeros_like(kv_prev)

    qkv = qkv_ref[...].astype(F32)
    qi = lax.broadcasted_iota(jnp.int32, (wdw, 2 * wdw), 0)
    kj = lax.broadcasted_iota(jnp.int32, (wdw, 2 * wdw), 1)
    rel = qi + wdw - kj
    in_window = (rel >= 0) & (rel < wdw)
    first = jnp.where(pl.program_id(1) > 0, 0, wdw)
    lo = lax.broadcasted_iota(jnp.int32, (wdw, LANES), 1) < HEAD_DIM
    lo2 = lax.broadcasted_iota(jnp.int32, (2 * wdw, LANES), 1) < HEAD_DIM
    rep = N_HEADS // ATT_KV_HEADS
    scale = HEAD_DIM ** -0.5
    bands, scores = [], []
    prev = kv_prev[...]
    for sb in range(nb):
        rs = slice(sb * wdw, (sb + 1) * wdw)
        kv = qkv[rs, WIDTH:]
        band = jnp.concatenate([prev, kv], axis=0)
        prev = kv
        bands.append(band)
        valid = in_window & (kj >= first) if sb == 0 else in_window
        for h in range(N_HEADS):
            g = h // rep
            qp = qkv[rs, (h // 2) * LANES:(h // 2 + 1) * LANES]
            qh = jnp.where(lo, qp, 0.0) if h % 2 == 0 else jnp.where(lo, 0.0, qp)
            s = _dot_nt(qh, band[:, g * LANES:(g + 1) * LANES]) * scale
            scores.append(jnp.where(valid, s, -1e30))
    kv_prev[...] = prev
    probs = []
    for i, s in enumerate(scores):
        sink = sink_ref[layer, i % N_HEADS]
        m = jnp.maximum(jnp.max(s, axis=-1, keepdims=True), sink)
        p = jnp.exp(s - m)
        den = jnp.sum(p, axis=-1, keepdims=True) + jnp.exp(sink - m)
        probs.append(p / den)
    for sb in range(nb):
        outs = []
        for tile in range(N_HEADS // 2):
            g = (2 * tile) // rep
            vb = bands[sb][:, (ATT_KV_HEADS + g) * LANES:(ATT_KV_HEADS + g + 1) * LANES]
            outs.append(_dot(probs[sb * N_HEADS + 2 * tile], jnp.where(lo2, vb, 0.0))
                        + _dot(probs[sb * N_HEADS + 2 * tile + 1], jnp.where(lo2, 0.0, vb)))
        y_ref[sb * wdw:(sb + 1) * wdw, :] = jnp.concatenate(outs, axis=1).astype(BF16)


def swa_mixer(proj, sinks, l, batch, seq, nb=4):
    n = proj.shape[0]
    rows = nb * WINDOW
    nt = seq // rows
    row = lambda b, t: b * nt + t
    return pl.pallas_call(
        functools.partial(_swa_kernel, nb=nb, layer=l),
        out_shape=jax.ShapeDtypeStruct((n, WIDTH), BF16),
        grid=(batch, nt),
        in_specs=[pl.BlockSpec(memory_space=pltpu.SMEM),
                  pl.BlockSpec((rows, 2 * WIDTH), lambda b, t: (row(b, t), C_ATT // (2 * WIDTH)))],
        out_specs=pl.BlockSpec((rows, WIDTH), lambda b, t: (row(b, t), 0)),
        scratch_shapes=[pltpu.VMEM((WINDOW, WIDTH), F32)],
        compiler_params=_params("parallel", "arbitrary"),
        name="swa",
    )(sinks, proj)


def _merge_ffn_kernel(x_ref, yrw, yssm, yatt, gmix, wgate, gb, wrw, wssm, watt, wout, gamma, wgu, wd,
                      gfin, o_ref, act, *, chunk, final_norm):
    x_in = x_ref[...]
    xm = _rms(x_in, gmix[...]).astype(BF16)
    merged = None
    for i, (y, w) in enumerate(((yrw, wrw), (yssm, wssm), (yatt, watt))):
        cols = slice(i * D_MODEL, (i + 1) * D_MODEL)
        gate = _sigmoid(jnp.dot(xm, wgate[:, cols], preferred_element_type=F32) + gb[:, cols])
        t = gate * jnp.dot(y[...], w[...], preferred_element_type=F32)
        merged = t if merged is None else merged + t
    x = x_in + _dot(merged, wout[...])
    xn = _rms(x, gamma[...]).astype(BF16)
    for c0 in range(0, FFN_HIDDEN, chunk):
        gate = jnp.dot(xn, wgu[:, c0:c0 + chunk], preferred_element_type=F32)
        up = jnp.dot(xn, wgu[:, FFN_HIDDEN + c0:FFN_HIDDEN + c0 + chunk], preferred_element_type=F32)
        act[:, c0:c0 + chunk] = (gate * _sigmoid(gate) * up).astype(BF16)
    out = x + jnp.dot(act[...], wd[...], preferred_element_type=F32)
    if final_norm:
        out = _rms(out, gfin[...])
    o_ref[...] = out


def merge_ffn(x, y_rw, y_ssm, y_att, prm, l, final_norm, tm=512, chunk=256):
    n = x.shape[0]
    names = ["norm_mix", "w_gate", "gate_b", "w_br_rw", "w_br_ssm", "w_br_att", "w_out", "norm_ffn",
             "w_gu", "w_down"]
    specs, args = _layer_specs(prm, names, l)
    fspecs, fargs = _layer_specs(prm, ["norm_final"], 0)
    rows = lambda width: pl.BlockSpec((tm, width), lambda i: (i, 0))
    return pl.pallas_call(
        functools.partial(_merge_ffn_kernel, chunk=chunk, final_norm=final_norm),
        out_shape=jax.ShapeDtypeStruct((n, D_MODEL), F32),
        grid=(n // tm,),
        in_specs=[rows(D_MODEL), rows(WIDTH), rows(WIDTH), rows(WIDTH)] + specs + fspecs,
        out_specs=rows(D_MODEL),
        scratch_shapes=[pltpu.VMEM((tm, FFN_HIDDEN), BF16)],
        compiler_params=_params("parallel"),
        name="merge_ffn_final" if final_norm else "merge_ffn",
    )(x, y_rw, y_ssm, y_att, *args, *fargs)


def _pad_axis(w, size, axis, at=0):
    pads = [(0, 0)] * w.ndim
    pads[axis] = (at, size - at - w.shape[axis])
    return jnp.pad(w, pads)


def _prep_params(p):
    depth = p["w_in"].shape[0]
    rw_cols = 3 * WIDTH + RW_DECAY_LORA + RW_ICLR_LORA + RW_GATE_LORA
    ssm_cols = WIDTH + (WIDTH + 2 * SSM_GROUPS * SSM_STATE) + N_HEADS
    att_cols = WIDTH + 2 * ATT_KV_HEADS * HEAD_DIM
    o_ssm = rw_cols
    o_att = o_ssm + ssm_cols
    o_gate = o_att + att_cols
    o_k = o_att + WIDTH
    o_v = o_k + ATT_KV_HEADS * HEAD_DIM
    w = p["w_in"].astype(BF16)
    zeros = lambda c: jnp.zeros((depth, D_MODEL, c), w.dtype)
    dup = lambda o: [w[:, :, o + HEAD_DIM * (i // 2):o + HEAD_DIM * (i // 2 + 1)] for i in range(4)]
    lv = jnp.concatenate([zeros(RW_VRES_LORA)[:1], p["rw_vres_down"].astype(BF16)], axis=0)
    pad_lglv = 256 - RW_GATE_LORA - RW_VRES_LORA
    w_cat = jnp.concatenate(
        [w[:, :, :3 * WIDTH],
         w[:, :, o_ssm:o_ssm + 3 * WIDTH],
         w[:, :, o_att:o_att + WIDTH]] + dup(o_k) + dup(o_v)
        + [w[:, :, 3 * WIDTH + 128:rw_cols], lv, zeros(pad_lglv),
           w[:, :, 3 * WIDTH:3 * WIDTH + 128],
           w[:, :, o_ssm + 3 * WIDTH:o_ssm + 3 * WIDTH + N_HEADS], zeros(128 - N_HEADS)], axis=2)
    assert w_cat.shape == (depth, D_MODEL, IN_COLS_PAD)

    row = lambda v: v.reshape(v.shape[0], 1, -1).astype(F32)
    mu = p["rw_mu"]
    vmu = jnp.concatenate([jnp.zeros((1, RW_VRES_LORA), F32), p["rw_vres_mu"]], axis=0)
    head_of = jnp.arange(WIDTH) // HEAD_DIM
    return {
        "norm_mix": row(p["norm_mix"]),
        "w_cat": w_cat,
        "w_gate": w[:, :, o_gate:o_gate + 3 * D_MODEL],
        "mu_rkv": row(mu[:, :3 * WIDTH]),
        "mu_lwla": row(mu[:, 3 * WIDTH:3 * WIDTH + 128]),
        "mu_lglv": row(jnp.concatenate([mu[:, 3 * WIDTH + 128:], vmu,
                                        jnp.zeros((depth, pad_lglv), F32)], axis=1)),
        "w0": row(p["rw_w0"]),
        "w2p": _pad_axis(p["rw_w2"], 128, 1, 0).astype(BF16),
        "a0": row(p["rw_a0"]),
        "a2p": _pad_axis(p["rw_a2"], 128, 1, RW_DECAY_LORA).astype(BF16),
        "g2p": _pad_axis(p["rw_g2"], 256, 1, 0).astype(BF16),
        "kkw": row(p["rw_k_k"]), "kaw": row(p["rw_k_a"]), "rkw": row(p["rw_r_k"]),
        "gnw": row(p["rw_gn_w"]), "gnb": row(p["rw_gn_b"]),
        "v0": row(p["rw_vres_v0"]),
        "v2p": _pad_axis(p["rw_vres_v2"], 256, 1, RW_GATE_LORA).astype(BF16),
        "hsum": (head_of[:, None] == head_of[None, :]).astype(BF16)[None],
        "convw": p["ssm_conv_w"].astype(F32),
        "convb": row(p["ssm_conv_b"]),
        "dtb": row(_pad_axis(p["ssm_dt_bias"], 128, 1)),
        "a_neg": row(_pad_axis(-jnp.exp(p["ssm_a_log"].astype(F32)), 128, 1)),
        "dskip": row(jnp.repeat(p["ssm_d"], HEAD_DIM, axis=1)),
        "normw": row(p["ssm_norm_w"]),
        "exp_ch": (jnp.arange(128)[:, None] == head_of[None, :]).astype(BF16)[None],
        "exp_ln": (jnp.arange(128)[:, None]
                   == (jnp.arange(N_HEADS * LANES) // LANES)[None, :]).astype(BF16)[None],
        "sinks": p["att_sinks"].astype(F32),
        "gate_b": row(p["gate_b"]),
        "w_br_rw": p["w_br_rw"].astype(BF16),
        "w_br_ssm": p["w_br_ssm"].astype(BF16),
        "w_br_att": p["w_br_att"].astype(BF16),
        "w_out": p["w_out"].astype(BF16),
        "norm_ffn": row(p["norm_ffn"]),
        "w_gu": p["ffn_w_gu"].astype(BF16),
        "w_down": p["ffn_w_down"].astype(BF16),
        "norm_final": p["norm_final"].reshape(1, 1, -1).astype(F32),
    }


def kernel(x, norm_mix, w_in, rw_mu, rw_w0, rw_w2, rw_a0, rw_a2, rw_g2, rw_k_k, rw_k_a, rw_r_k, rw_gn_w, rw_gn_b, rw_vres_down, rw_vres_mu, rw_vres_v0, rw_vres_v2, ssm_conv_w, ssm_conv_b, ssm_dt_bias, ssm_a_log, ssm_d, ssm_norm_w, att_sinks, gate_b, w_br_rw, w_br_ssm, w_br_att, w_out, norm_ffn, ffn_w_gu, ffn_w_down, norm_final):
    p = dict(norm_mix=norm_mix, w_in=w_in, rw_mu=rw_mu, rw_w0=rw_w0, rw_w2=rw_w2, rw_a0=rw_a0,
             rw_a2=rw_a2, rw_g2=rw_g2, rw_k_k=rw_k_k, rw_k_a=rw_k_a,
             rw_r_k=rw_r_k.reshape(rw_r_k.shape[0], -1), rw_gn_w=rw_gn_w, rw_gn_b=rw_gn_b,
             rw_vres_down=rw_vres_down, rw_vres_mu=rw_vres_mu, rw_vres_v0=rw_vres_v0,
             rw_vres_v2=rw_vres_v2, ssm_conv_w=ssm_conv_w, ssm_conv_b=ssm_conv_b,
             ssm_dt_bias=ssm_dt_bias, ssm_a_log=ssm_a_log, ssm_d=ssm_d, ssm_norm_w=ssm_norm_w,
             att_sinks=att_sinks, gate_b=gate_b, w_br_rw=w_br_rw, w_br_ssm=w_br_ssm,
             w_br_att=w_br_att, w_out=w_out, norm_ffn=norm_ffn, ffn_w_gu=ffn_w_gu,
             ffn_w_down=ffn_w_down, norm_final=norm_final)
    batch, seq, _ = x.shape
    depth = w_in.shape[0]
    prm = _prep_params(p)
    xf = x.reshape(batch * seq, D_MODEL)
    vfirst = None
    for l in range(depth):
        proj = in_proj(xf, prm, l)
        if l == 0:
            y_rw, vfirst = rwkv_mixer(proj, None, prm, l, batch, seq)
        else:
            y_rw = rwkv_mixer(proj, vfirst, prm, l, batch, seq)
        y_ssm = ssd_mixer(proj, prm, l, batch, seq)
        y_att = swa_mixer(proj, prm["sinks"], l, batch, seq)
        xf = merge_ffn(xf, y_rw, y_ssm, y_att, prm, l, final_norm=(l == depth - 1))
    return xf.reshape(batch, seq, D_MODEL)
```
